```python
import math
import jax
import jax.numpy as jnp
from jax import lax
import numpy as np

D_MODEL = 1024
BATCH = 2
SEQ = 16384
DEPTH = 2

GRID_W = 64
CTX_LEN = 256
F32 = jnp.float32
EPS = 1e-6

N_MIXERS = 4
D_GROUP = D_MODEL // N_MIXERS

POOL_WINDOWS = (2, 4, 8, 16)
POOL_CH = D_GROUP // len(POOL_WINDOWS)

HEAD_DIM = 64
N_Q_HEADS = D_GROUP // HEAD_DIM
N_KV_HEADS = N_Q_HEADS // 2
Q_GROUP = N_Q_HEADS // N_KV_HEADS
ROPE_AXIS_DIM = HEAD_DIM // 2
ROPE_FREQS = ROPE_AXIS_DIM // 2
ROPE_THETA = 10000.0
Q_BLOCK = 128
ATTN_SCALE = HEAD_DIM ** -0.5

LRU_WIDTH = D_GROUP
LRU_BLOCKS = 4
LRU_BLOCK_W = LRU_WIDTH // LRU_BLOCKS
LRU_CONV = 4
LRU_C = 8.0

HY_WIDTH = D_GROUP
HY_ORDER = 2
HY_SHORT = 3
HY_EMB = 33
HY_BANDS = (HY_EMB - 1) // 2
HY_HIDDEN = 64
HY_TARGET = 1e-2
HY_FAST = 0.3
HY_SLOW = 1.5
HY_MIN_DECAY = math.log(HY_TARGET) / HY_SLOW
HY_MAX_DECAY = math.log(HY_TARGET) / HY_FAST

D_FF = 4 * D_MODEL

Q_COLS = N_Q_HEADS * HEAD_DIM
KV_COLS = N_KV_HEADS * HEAD_DIM
HY_COLS = (HY_ORDER + 1) * HY_WIDTH
COL_SPLITS = (D_GROUP,
              D_GROUP + Q_COLS,
              D_GROUP + Q_COLS + KV_COLS,
              D_GROUP + Q_COLS + 2 * KV_COLS,
              D_GROUP + Q_COLS + 2 * KV_COLS + LRU_WIDTH,
              D_GROUP + Q_COLS + 2 * KV_COLS + 2 * LRU_WIDTH)
IN_COLS = COL_SPLITS[-1] + HY_COLS

kernel_name = 'hybrid_pool_gqa_rglru_hyena_dit'


def rmsnorm(x, g):
    xf = x.astype(F32)
    y = xf * lax.rsqrt(jnp.mean(xf * xf, axis=-1, keepdims=True) + EPS)
    return (y * g.astype(F32)).astype(x.dtype)


def modulate(h, shift, scale):
    return h * (1.0 + scale) + shift


def depthwise_conv(u, w, b, left):
    width = w.shape[0]
    L = u.shape[1]
    up = jnp.pad(u, ((0, 0), (left, width - 1 - left), (0, 0)))
    out = b
    for j in range(width):
        out = out + up[:, j:j + L] * w[j]
    return out


def pool_mixer(u, w, scale):
    L = u.shape[1]
    uf = u.astype(F32)
    cs = jnp.pad(jnp.cumsum(uf, axis=1), ((0, 0), (1, 0), (0, 0)))
    t = jnp.arange(L)
    outs = []
    for gi, win in enumerate(POOL_WINDOWS):
        sl = slice(gi * POOL_CH, (gi + 1) * POOL_CH)
        lo = jnp.clip(t - win // 2, 0, L)
        hi = jnp.clip(t + win - win // 2, 0, L)
        csg = cs[..., sl]
        mean = (csg[:, hi] - csg[:, lo]) / (hi - lo).astype(F32)[None, :, None]
        d = (mean - uf[..., sl]).astype(u.dtype)
        outs.append(d @ w[gi])
    return jnp.concatenate(outs, axis=-1) * scale


def axial_rope_tables(rows):
    row = jnp.repeat(jnp.arange(rows), GRID_W).astype(F32)
    col = jnp.tile(jnp.arange(GRID_W), rows).astype(F32)
    inv = ROPE_THETA ** (-jnp.arange(ROPE_FREQS, dtype=F32) / ROPE_FREQS)
    ang = jnp.stack([row[:, None] * inv, col[:, None] * inv], axis=1)
    return jnp.cos(ang), jnp.sin(ang)


def apply_axial_rope(t, cos, sin):
    B, L, H, _ = t.shape
    ts = t.reshape(B, L, H, 2, 2, ROPE_FREQS)
    t1, t2 = ts[..., 0, :], ts[..., 1, :]
    c = cos[None, :, None]
    s = sin[None, :, None]
    out = jnp.stack([t1 * c - t2 * s, t2 * c + t1 * s], axis=-2)
    return out.reshape(B, L, H, HEAD_DIM).astype(t.dtype)


def attend(q, k, v):
    s = jnp.einsum('bkgqd,bksd->bkgqs', q, k).astype(F32) * ATTN_SCALE
    p = jax.nn.softmax(s, axis=-1)
    return jnp.einsum('bkgqs,bksd->bkgqd', p.astype(v.dtype), v)


def attention_mixer(qx, kx, vx, qc, kc, vc, gq, gk, cos_t, sin_t, need_ctx):
    B, L = qx.shape[0], qx.shape[1]
    C = qc.shape[1]
    qx = apply_axial_rope(rmsnorm(qx.reshape(B, L, N_Q_HEADS, HEAD_DIM), gq), cos_t, sin_t)
    kx = apply_axial_rope(rmsnorm(kx.reshape(B, L, N_KV_HEADS, HEAD_DIM), gk), cos_t, sin_t)
    vx = vx.reshape(B, L, N_KV_HEADS, HEAD_DIM)
    kc = rmsnorm(kc.reshape(B, C, N_KV_HEADS, HEAD_DIM), gk)
    vc = vc.reshape(B, C, N_KV_HEADS, HEAD_DIM)
    k_all = jnp.concatenate([kc, kx], axis=1).transpose(0, 2, 1, 3)
    v_all = jnp.concatenate([vc, vx], axis=1).transpose(0, 2, 1, 3)
    nblk = L // Q_BLOCK
    qb = qx.reshape(B, nblk, Q_BLOCK, N_KV_HEADS, Q_GROUP, HEAD_DIM).transpose(1, 0, 3, 4, 2, 5)
    ob = lax.map(lambda q: attend(q, k_all, v_all), qb)
    ox = ob.transpose(1, 0, 4, 2, 3, 5).reshape(B, L, D_GROUP)
    if not need_ctx:
        return ox, None
    qc = rmsnorm(qc.reshape(B, C, N_Q_HEADS, HEAD_DIM), gq)
    qcb = qc.reshape(B, C, N_KV_HEADS, Q_GROUP, HEAD_DIM).transpose(0, 2, 3, 1, 4)
    oc = attend(qcb, kc.transpose(0, 2, 1, 3), vc.transpose(0, 2, 1, 3))
    oc = oc.transpose(0, 3, 1, 2, 4).reshape(B, C, D_GROUP)
    return ox, oc


def rglru_coeffs(xc, wa, b_r, wx, b_i, lam):
    xf = xc.astype(F32)
    xb = xf.reshape(xf.shape[0], xf.shape[1], LRU_BLOCKS, LRU_BLOCK_W)
    r = jax.nn.sigmoid(jnp.einsum('blnc,ncd->blnd', xb, wa.astype(F32)).reshape(xf.shape) + b_r)
    i = jax.nn.sigmoid(jnp.einsum('blnc,ncd->blnd', xb, wx.astype(F32)).reshape(xf.shape) + b_i)
    log_a = -LRU_C * r * jax.nn.softplus(-lam.astype(F32))
    a = jnp.exp(log_a)
    b = jnp.sqrt(-jnp.expm1(2.0 * log_a)) * (i * xf)
    return a, b


def linear_scan(a, b, h0):
    b = b.at[:, 0].add(a[:, 0] * h0)

    def combine(l, r):
        return (l[0] * r[0], r[0] * l[1] + r[1])

    _, h = lax.associative_scan(combine, (a, b), axis=1)
    return h


def rglru_mixer(ux, gx, uc, gc, conv_w, conv_b, wa, b_r, wx, b_i, lam, need_ctx):
    xcx = depthwise_conv(ux, conv_w, conv_b, LRU_CONV // 2)
    xcc = depthwise_conv(uc, conv_w, conv_b, LRU_CONV // 2)
    B = ux.shape[0]
    hx_sum = jnp.zeros(ux.shape, F32)
    hc_sum = jnp.zeros(uc.shape, F32)
    for d in range(2):
        ax, bx = rglru_coeffs(xcx, wa[d], b_r[d], wx[d], b_i[d], lam[d])
        ac, bc = rglru_coeffs(xcc, wa[d], b_r[d], wx[d], b_i[d], lam[d])
        if d == 1:
            ax, bx, ac, bc = ax[:, ::-1], bx[:, ::-1], ac[:, ::-1], bc[:, ::-1]
        hc = linear_scan(ac, bc, jnp.zeros((B, LRU_WIDTH), F32))
        hx = linear_scan(ax, bx, hc[:, -1])
        if d == 1:
            hc, hx = hc[:, ::-1], hx[:, ::-1]
        hx_sum = hx_sum + hx
        hc_sum = hc_sum + hc
    yx = (hx_sum * jax.nn.gelu(gx.astype(F32))).astype(ux.dtype)
    if not need_ctx:
        return yx, None
    yc = (hc_sum * jax.nn.gelu(gc.astype(F32))).astype(uc.dtype)
    return yx, yc


def hyena_filter_spectrum(L, w1, b1, freq, w2, b2, w3):
    t01 = jnp.linspace(0.0, 1.0, L, dtype=F32)[:, None]
    w = 2.0 * math.pi * jnp.arange(L, dtype=F32) / L
    bands = jnp.linspace(1e-4, HY_BANDS - 1, HY_BANDS, dtype=F32)
    z = w[:, None] * bands[None, :]
    emb = jnp.concatenate([t01, jnp.cos(z), -jnp.sin(z)], axis=-1)
    fq = freq.astype(F32)
    h = jnp.sin(fq * (emb @ w1.astype(F32) + b1.astype(F32)))
    h = jnp.sin(fq * (h @ w2.astype(F32) + b2.astype(F32)))
    h = (h @ w3.astype(F32)).reshape(L, 2, HY_ORDER, HY_WIDTH)
    deltas = jnp.abs(jnp.linspace(HY_MIN_DECAY, HY_MAX_DECAY, HY_WIDTH, dtype=F32))
    h = h * jnp.exp(-t01 * deltas)[:, None, None, :]
    filt = jnp.concatenate([h[:, 0], jnp.zeros((1, HY_ORDER, HY_WIDTH), F32), h[:0:-1, 1]], axis=0)
    filt = filt / jnp.sum(jnp.abs(filt), axis=0, keepdims=True)
    return jnp.fft.rfft(filt, axis=0)


def fft_long_conv(u, spec, skip):
    L = u.shape[1]
    uf = u.astype(F32)
    U = jnp.fft.rfft(uf, n=2 * L, axis=1)
    y = jnp.fft.irfft(U * spec[None], n=2 * L, axis=1)[:, :L]
    return (y + uf * skip.astype(F32)).astype(u.dtype)


def hyena_mixer(u, conv_w, conv_b, w1, b1, freq, w2, b2, w3, skip):
    L = u.shape[1]
    z = depthwise_conv(u, conv_w, conv_b, HY_SHORT // 2)
    x1, x2, v = jnp.split(z, 3, axis=-1)
    spec = hyena_filter_spectrum(L, w1, b1, freq, w2, b2, w3)
    y = x1 * fft_long_conv(v, spec[:, 0], skip[0])
    y = x2 * fft_long_conv(y, spec[:, 1], skip[1])
    return y


def sq_relu_mlp(h, w1, w2):
    return jnp.square(jax.nn.relu(h @ w1)) @ w2


def setup_inputs(seed: int = 0) -> dict:
    key = jax.random.key(seed)
    keys = list(jax.random.split(key, 40))

    def nrm(shape, scale):
        return jax.random.normal(keys.pop(), shape, F32) * scale

    a8 = jax.random.uniform(keys.pop(), (DEPTH, 2, LRU_WIDTH), F32, minval=0.9, maxval=0.999)
    a = a8 ** (1.0 / LRU_C)
    lru_lambda = jnp.log(a) - jnp.log1p(-a)
    return {
        'x': nrm((BATCH, SEQ, D_MODEL), 1.0),
        'c': nrm((BATCH, D_MODEL), 1.0),
        'ctx': nrm((BATCH, CTX_LEN, D_MODEL), 1.0),
        'c_ctx': nrm((D_MODEL,), 1.0),
        'w_mod': nrm((DEPTH, D_MODEL, 6 * D_MODEL), 0.5 * D_MODEL ** -0.5),
        'b_mod': nrm((DEPTH, 6 * D_MODEL), 0.02),
        'g_pre_mix': 1.0 + nrm((DEPTH, D_MODEL), 0.05),
        'g_post_mix': 1.0 + nrm((DEPTH, D_MODEL), 0.05),
        'g_pre_mlp': 1.0 + nrm((DEPTH, D_MODEL), 0.05),
        'g_post_mlp': 1.0 + nrm((DEPTH, D_MODEL), 0.05),
        'w_in': nrm((DEPTH, D_MODEL, IN_COLS), D_MODEL ** -0.5),
        'w_out': nrm((DEPTH, D_MODEL, D_MODEL), D_MODEL ** -0.5),
        'pool_w': nrm((DEPTH, len(POOL_WINDOWS), POOL_CH, POOL_CH), POOL_CH ** -0.5),
        'pool_scale': 1.0 + nrm((DEPTH, D_GROUP), 0.1),
        'q_norm_g': 1.0 + nrm((DEPTH, HEAD_DIM), 0.05),
        'k_norm_g': 1.0 + nrm((DEPTH, HEAD_DIM), 0.05),
        'lru_conv_w': nrm((DEPTH, LRU_CONV, LRU_WIDTH), LRU_CONV ** -0.5),
        'lru_conv_b': nrm((DEPTH, LRU_WIDTH), 0.02),
        'lru_wa': nrm((DEPTH, 2, LRU_BLOCKS, LRU_BLOCK_W, LRU_BLOCK_W), LRU_BLOCK_W ** -0.5),
        'lru_ba': nrm((DEPTH, 2, LRU_WIDTH), 0.1),
        'lru_wx': nrm((DEPTH, 2, LRU_BLOCKS, LRU_BLOCK_W, LRU_BLOCK_W), LRU_BLOCK_W ** -0.5),
        'lru_bx': nrm((DEPTH, 2, LRU_WIDTH), 0.1),
        'lru_lambda': lru_lambda,
        'hy_conv_w': nrm((DEPTH, HY_SHORT, HY_COLS), HY_SHORT ** -0.5),
        'hy_conv_b': nrm((DEPTH, HY_COLS), 0.02),
        'hy_w1': nrm((DEPTH, HY_EMB, HY_HIDDEN), HY_EMB ** -0.5),
        'hy_b1': nrm((DEPTH, HY_HIDDEN), 0.1),
        'hy_freq': 1.0 + nrm((DEPTH, HY_HIDDEN), 0.05),
        'hy_w2': nrm((DEPTH, HY_HIDDEN, HY_HIDDEN), HY_HIDDEN ** -0.5),
        'hy_b2': nrm((DEPTH, HY_HIDDEN), 0.1),
        'hy_w3': nrm((DEPTH, HY_HIDDEN, 2 * HY_ORDER * HY_WIDTH), HY_HIDDEN ** -0.5),
        'hy_skip': nrm((DEPTH, HY_ORDER, HY_WIDTH), 0.5),
        'mlp_w1': nrm((DEPTH, D_MODEL, D_FF), D_MODEL ** -0.5),
        'mlp_w2': nrm((DEPTH, D_FF, D_MODEL), D_FF ** -0.5),
    }


def reference(x, c, ctx, c_ctx, w_mod, b_mod, g_pre_mix, g_post_mix, g_pre_mlp, g_post_mlp,
              w_in, w_out, pool_w, pool_scale, q_norm_g, k_norm_g,
              lru_conv_w, lru_conv_b, lru_wa, lru_ba, lru_wx, lru_bx, lru_lambda,
              hy_conv_w, hy_conv_b, hy_w1, hy_b1, hy_freq, hy_w2, hy_b2, hy_w3, hy_skip,
              mlp_w1, mlp_w2):
    L = x.shape[1]
    rows = L // GRID_W
    cos_t, sin_t = axial_rope_tables(rows)
    cx = ctx
    cond_x = jax.nn.silu(c)
    cond_c = jax.nn.silu(c_ctx)
    for l in range(DEPTH):
        need_ctx = l < DEPTH - 1
        mod_x = (cond_x @ w_mod[l] + b_mod[l])[:, None, :]
        mod_c = (cond_c @ w_mod[l] + b_mod[l])[None, None, :]
        sh1x, sc1x, ga1x, sh2x, sc2x, ga2x = jnp.split(mod_x, 6, axis=-1)
        sh1c, sc1c, ga1c, sh2c, sc2c, ga2c = jnp.split(mod_c, 6, axis=-1)

        px = jnp.split(modulate(rmsnorm(x, g_pre_mix[l]), sh1x, sc1x) @ w_in[l], COL_SPLITS, axis=-1)
        pc = jnp.split(modulate(rmsnorm(cx, g_pre_mix[l]), sh1c, sc1c) @ w_in[l], COL_SPLITS, axis=-1)

        ya_x = pool_mixer(px[0], pool_w[l], pool_scale[l])
        yb_x, yb_c = attention_mixer(px[1], px[2], px[3], pc[1], pc[2], pc[3],
                                     q_norm_g[l], k_norm_g[l], cos_t, sin_t, need_ctx)
        yc_x, yc_c = rglru_mixer(px[4], px[5], pc[4], pc[5], lru_conv_w[l], lru_conv_b[l],
                                 lru_wa[l], lru_ba[l], lru_wx[l], lru_bx[l], lru_lambda[l], need_ctx)
        yd_x = hyena_mixer(px[6], hy_conv_w[l], hy_conv_b[l], hy_w1[l], hy_b1[l], hy_freq[l],
                           hy_w2[l], hy_b2[l], hy_w3[l], hy_skip[l])

        mix_x = jnp.concatenate([ya_x, yb_x, yc_x, yd_x], axis=-1) @ w_out[l]
        x = x + ga1x * rmsnorm(mix_x, g_post_mix[l])
        h_x = modulate(rmsnorm(x, g_pre_mlp[l]), sh2x, sc2x)
        x = x + ga2x * rmsnorm(sq_relu_mlp(h_x, mlp_w1[l], mlp_w2[l]), g_post_mlp[l])

        if need_ctx:
            ya_c = pool_mixer(pc[0], pool_w[l], pool_scale[l])
            yd_c = hyena_mixer(pc[6], hy_conv_w[l], hy_conv_b[l], hy_w1[l], hy_b1[l], hy_freq[l],
                               hy_w2[l], hy_b2[l], hy_w3[l], hy_skip[l])
            mix_c = jnp.concatenate([ya_c, yb_c, yc_c, yd_c], axis=-1) @ w_out[l]
            cx = cx + ga1c * rmsnorm(mix_c, g_post_mix[l])
            h_c = modulate(rmsnorm(cx, g_pre_mlp[l]), sh2c, sc2c)
            cx = cx + ga2c * rmsnorm(sq_relu_mlp(h_c, mlp_w1[l], mlp_w2[l]), g_post_mlp[l])
    return x
```

```python
import functools
import math

import jax
import jax.numpy as jnp
from jax import lax
from jax.experimental import pallas as pl
from jax.experimental.pallas import tpu as pltpu

F32 = jnp.float32
BF16 = jnp.bfloat16
EPS = 1e-6

GRID_W = 64
HEAD_DIM = 64
N_KV_HEADS = 2
ROPE_FREQS = 16
ROPE_THETA = 10000.0
POOL_WINDOWS = (2, 4, 8, 16)
LRU_C = 8.0
HY_EMB = 33
HY_EMB_PAD = 64
HY_BANDS = 16
HY_MIN_DECAY = math.log(1e-2) / 1.5
HY_MAX_DECAY = math.log(1e-2) / 0.3
DFT_N2 = 128
HALO = 16
VMEM_LIMIT = 56 * 1024 * 1024
HIGHEST = lax.Precision.HIGHEST


def _params(*sem):
    return pltpu.CompilerParams(dimension_semantics=sem, vmem_limit_bytes=VMEM_LIMIT)


def _resident(shape, index_map):
    return pl.BlockSpec(shape, index_map, pipeline_mode=pl.Buffered(1))


def _rms(x, g):
    return x * lax.rsqrt(jnp.mean(x * x, axis=-1, keepdims=True) + EPS) * g


def _shift_rows(x, k):
    n = x.shape[0]
    return pltpu.roll(x, k % n, 0)


def _with_halo(prev_ref, cur_ref, next_ref, tile, ntiles):
    prev = jnp.where(tile > 0, prev_ref[...], 0.0)
    nxt = jnp.where(tile < ntiles - 1, next_ref[...], 0.0)
    return jnp.concatenate([prev, cur_ref[...], nxt], axis=0)


def _halo_specs(tm, width, col, ntiles, tile_of=lambda i: i, nlead=2):
    r = tm // HALO
    last = ntiles * r - 1

    def prev_map(*g):
        return (g[0], jnp.maximum(tile_of(g[nlead - 1]) * r - 1, 0), col)

    def cur_map(*g):
        return (g[0], tile_of(g[nlead - 1]), col)

    def next_map(*g):
        return (g[0], jnp.minimum((tile_of(g[nlead - 1]) + 1) * r, last), col)

    return [pl.BlockSpec((None, HALO, width), prev_map),
            pl.BlockSpec((None, tm, width), cur_map),
            pl.BlockSpec((None, HALO, width), next_map)]


def _mod_kernel(c_ref, w_ref, b_ref, o_ref):
    cond = c_ref[...]
    act = (cond * jax.nn.sigmoid(cond)).astype(BF16)
    o_ref[...] = jnp.dot(act, w_ref[...].astype(BF16), preferred_element_type=F32) + b_ref[...]


def _mod_call(cond, w_mod, b_mod):
    depth, d, n = w_mod.shape
    tn = 1024
    return pl.pallas_call(
        _mod_kernel,
        grid=(depth, n // tn),
        in_specs=[pl.BlockSpec((8, d), lambda l, j: (0, 0)),
                  pl.BlockSpec((None, d, tn), lambda l, j: (l, 0, j)),
                  pl.BlockSpec((None, 1, tn), lambda l, j: (l, 0, j))],
        out_specs=pl.BlockSpec((None, 8, tn), lambda l, j: (l, 0, j)),
        out_shape=jax.ShapeDtypeStruct((depth, 8, n), F32),
        compiler_params=_params("parallel", "parallel"),
        name="adaln_mod",
    )(cond, w_mod, b_mod.reshape(depth, 1, n))


def _inproj_kernel(x_ref, mod_ref, g_ref, w_ref, o_ref):
    m = mod_ref[...]
    h = _rms(x_ref[...], g_ref[...]) * (1.0 + m[1:2]) + m[0:1]
    o_ref[...] = jnp.dot(h.astype(BF16), w_ref[...], preferred_element_type=F32)


def _inproj_call(x, mod, g, w):
    b, l, d = x.shape
    n = w.shape[1]
    tm = min(l, 512)
    return pl.pallas_call(
        _inproj_kernel,
        grid=(b, l // tm),
        in_specs=[pl.BlockSpec((None, tm, d), lambda bi, i: (bi, i, 0)),
                  pl.BlockSpec((None, 6, d), lambda bi, i: (bi, 0, 0)),
                  pl.BlockSpec((1, d), lambda bi, i: (0, 0)),
                  _resident((d, n), lambda bi, i: (0, 0))],
        out_specs=pl.BlockSpec((None, tm, n), lambda bi, i: (bi, i, 0)),
        out_shape=jax.ShapeDtypeStruct((b, l, n), F32),
        compiler_params=_params("parallel", "parallel"),
        name="in_proj",
    )(x, mod, g.reshape(1, d), w)


def _pool_kernel(prev_ref, cur_ref, next_ref, w_ref, sc_ref, o_ref, *, seq_len):
    i = pl.program_id(1)
    tm, ch = cur_ref.shape
    e = _with_halo(prev_ref, cur_ref, next_ref, i, pl.num_programs(1))
    s2 = e + _shift_rows(e, 1)
    s4 = _shift_rows(s2, 1) + _shift_rows(s2, -1)
    s8 = _shift_rows(s4, 2) + _shift_rows(s4, -2)
    s16 = _shift_rows(s8, 4) + _shift_rows(s8, -4)
    t = i * tm + lax.broadcasted_iota(jnp.int32, (tm, 1), 0)
    group = lax.broadcasted_iota(jnp.int32, (1, ch), 1) // (ch // len(POOL_WINDOWS))
    mean = jnp.zeros((tm, ch), F32)
    for gi, (win, s) in enumerate(zip(POOL_WINDOWS, (s2, s4, s8, s16))):
        lo = jnp.maximum(t - win // 2, 0)
        hi = jnp.minimum(t + win - win // 2, seq_len)
        m = s[HALO:HALO + tm] / (hi - lo).astype(F32)
        mean = jnp.where(group == gi, m, mean)
    d = (mean - cur_ref[...]).astype(BF16)
    o_ref[...] = jnp.dot(d, w_ref[...], preferred_element_type=F32) * sc_ref[...]


def _pool_call(p, w_bd, scale):
    b, l, _ = p.shape
    ch = w_bd.shape[0]
    tm = min(l, 512)
    nt = l // tm
    return pl.pallas_call(
        functools.partial(_pool_kernel, seq_len=l),
        grid=(b, nt),
        in_specs=_halo_specs(tm, ch, 0, nt) + [
            pl.BlockSpec((ch, ch), lambda bi, i: (0, 0)),
            pl.BlockSpec((1, ch), lambda bi, i: (0, 0))],
        out_specs=pl.BlockSpec((None, tm, ch), lambda bi, i: (bi, i, 0)),
        out_shape=jax.ShapeDtypeStruct((b, l, ch), F32),
        compiler_params=_params("parallel", "parallel"),
        name="pool_mixer",
    )(p, p, p, w_bd, scale.reshape(1, ch))


def _head_norm(x, g, avg):
    sq = x * x
    hi = sq.astype(BF16)
    lo = (sq - hi.astype(F32)).astype(BF16)
    ms = (jnp.dot(hi, avg, preferred_element_type=F32) + jnp.dot(lo, avg, preferred_element_type=F32))
    return x * lax.rsqrt(ms + EPS) * g


def _rope(x, cos, sin):
    n = x.shape[1]
    lane = lax.broadcasted_iota(jnp.int32, (1, n), 1)
    partner = jnp.where(lane % 32 < 16, pltpu.roll(x, n - 16, 1), pltpu.roll(x, 16, 1))
    return x * cos + partner * sin


def _qkv_kernel(*refs, rope):
    if rope:
        q_ref, kv_ref, cos_ref, sin_ref, gq_ref, gk_ref, avg_ref, qo_ref, kt_ref, vo_ref = refs
    else:
        q_ref, kv_ref, gq_ref, gk_ref, avg_ref, qo_ref, kt_ref, vo_ref = refs
    avg = avg_ref[...]
    tm = q_ref.shape[0]
    for half in range(2):
        q = _head_norm(q_ref[:, half * 128:(half + 1) * 128], gq_ref[...], avg)
        if rope:
            q = _rope(q, cos_ref[...], sin_ref[...])
        qo_ref[:, half * 128:(half + 1) * 128] = (q * (HEAD_DIM ** -0.5)).astype(BF16)
    k = _head_norm(kv_ref[:, 0:128], gk_ref[...], avg)
    if rope:
        k = _rope(k, cos_ref[...], sin_ref[...])
    kt = k.T
    for h in range(N_KV_HEADS):
        kt_ref[h] = kt[h * HEAD_DIM:(h + 1) * HEAD_DIM].astype(BF16)
    vo_ref[...] = kv_ref[:, 128:256].astype(BF16)


def _qkv_call(p, cos, sin, gq, gk, avg):
    b, l, _ = p.shape
    tm = min(l, 512)
    rope = cos is not None
    tables = [cos, sin] if rope else []
    table_specs = [pl.BlockSpec((tm, 128), lambda bi, i: (i, 0))] * 2 if rope else []
    const = pl.BlockSpec((1, 128), lambda bi, i: (0, 0))
    return pl.pallas_call(
        functools.partial(_qkv_kernel, rope=rope),
        grid=(b, l // tm),
        in_specs=[pl.BlockSpec((None, tm, 256), lambda bi, i: (bi, i, 1)),
                  pl.BlockSpec((None, tm, 256), lambda bi, i: (bi, i, 2))] + table_specs + [
                  const, const, pl.BlockSpec((128, 128), lambda bi, i: (0, 0))],
        out_specs=[pl.BlockSpec((None, tm, 256), lambda bi, i: (bi, i, 0)),
                   pl.BlockSpec((None, N_KV_HEADS, HEAD_DIM, tm), lambda bi, i: (bi, 0, 0, i)),
                   pl.BlockSpec((None, tm, 128), lambda bi, i: (bi, i, 0))],
        out_shape=[jax.ShapeDtypeStruct((b, l, 256), BF16),
                   jax.ShapeDtypeStruct((b, N_KV_HEADS, HEAD_DIM, l), BF16),
                   jax.ShapeDtypeStruct((b, l, 128), BF16)],
        compiler_params=_params("parallel", "parallel"),
        name="qkv_prep",
    )(p, p, *tables, gq, gk, avg)


def _flash_kernel(*refs, with_ctx):
    if with_ctx:
        q_ref, kt_ref, v_ref, ktc_ref, vc_ref, o_ref, qs, m_s, l_s, acc = refs
    else:
        q_ref, kt_ref, v_ref, o_ref, qs, m_s, l_s, acc = refs
    h = pl.program_id(1)
    kj = pl.program_id(3)
    tq = q_ref.shape[0]

    def update(kt, v):
        s = jnp.dot(qs[...], kt, preferred_element_type=F32)
        m_prev = m_s[...]
        m_new = jnp.maximum(m_prev, jnp.max(s, axis=-1, keepdims=True))
        alpha = jnp.exp(m_prev - m_new)
        p = jnp.exp(s - m_new[:, 0:1])
        l_s[...] = alpha * l_s[...] + jnp.sum(p, axis=-1, keepdims=True)
        acc[...] = alpha * acc[...] + jnp.dot(p.astype(BF16), v, preferred_element_type=F32)
        m_s[...] = m_new

    @pl.when(kj == 0)
    def _():
        qs[0:tq] = q_ref[:, 0:HEAD_DIM]
        qs[tq:2 * tq] = q_ref[:, HEAD_DIM:2 * HEAD_DIM]
        m_s[...] = jnp.full(m_s.shape, -1e30, F32)
        l_s[...] = jnp.zeros(l_s.shape, F32)
        acc[...] = jnp.zeros(acc.shape, F32)
        if with_ctx:
            update(ktc_ref[...], vc_ref[...])

    update(kt_ref[...], v_ref[...])

    @pl.when(kj == pl.num_programs(3) - 1)
    def _():
        o = acc[...] / l_s[...]
        top, bot = o[0:tq], o[tq:2 * tq]
        top = jnp.where(h == 0, top, pltpu.roll(top, HEAD_DIM, 1))
        bot = jnp.where(h == 0, pltpu.roll(bot, HEAD_DIM, 1), bot)
        lane = lax.broadcasted_iota(jnp.int32, (1, 2 * HEAD_DIM), 1)
        o_ref[...] = jnp.where(lane < HEAD_DIM, top, bot).astype(o_ref.dtype)


def _flash_call(q, kt, v, ktc=None, vc=None):
    b, lq, _ = q.shape
    lk = kt.shape[-1]
    tq = min(lq, 512)
    tk = min(lk, 1024)
    with_ctx = ktc is not None
    in_specs = [pl.BlockSpec((None, tq, 128), lambda bi, h, i, j: (bi, i, h)),
                pl.BlockSpec((None, None, HEAD_DIM, tk), lambda bi, h, i, j: (bi, h, 0, j)),
                pl.BlockSpec((None, tk, 128), lambda bi, h, i, j: (bi, j, 0))]
    args = [q, kt, v]
    if with_ctx:
        c = ktc.shape[-1]
        in_specs += [pl.BlockSpec((None, None, HEAD_DIM, c), lambda bi, h, i, j: (bi, h, 0, 0)),
                     pl.BlockSpec((None, c, 128), lambda bi, h, i, j: (bi, 0, 0))]
        args += [ktc, vc]
    return pl.pallas_call(
        functools.partial(_flash_kernel, with_ctx=with_ctx),
        grid=(b, N_KV_HEADS, lq // tq, lk // tk),
        in_specs=in_specs,
        out_specs=pl.BlockSpec((None, tq, 128), lambda bi, h, i, j: (bi, i, h)),
        out_shape=jax.ShapeDtypeStruct((b, lq, 256), BF16),
        scratch_shapes=[pltpu.VMEM((2 * tq, HEAD_DIM), BF16),
                        pltpu.VMEM((2 * tq, 1), F32),
                        pltpu.VMEM((2 * tq, 1), F32),
                        pltpu.VMEM((2 * tq, 128), F32)],
        compiler_params=_params("parallel", "parallel", "parallel", "arbitrary"),
        name="flash_attention",
    )(*args)


def _lru_kernel(prev_ref, cur_ref, next_ref, cw_ref, cb_ref, wa_ref, br_ref, wx_ref, bi_ref, lam_ref, h0_ref,
                o_ref, a_s, b_s, carry, *, reverse):
    i = pl.program_id(1)
    nt = pl.num_programs(1)
    tm, ch = cur_ref.shape
    tile = nt - 1 - i if reverse else i

    @pl.when(i == 0)
    def _():
        carry[...] = jnp.broadcast_to(h0_ref[...], carry.shape)

    e = _with_halo(prev_ref, cur_ref, next_ref, tile, nt)
    cw = cw_ref[...]
    xc = cb_ref[...] + sum(_shift_rows(e, 2 - j) * cw[j:j + 1] for j in range(4))
    xc = xc[HALO:HALO + tm]
    xb = xc.astype(BF16)
    r = jax.nn.sigmoid(jnp.dot(xb, wa_ref[...], preferred_element_type=F32) + br_ref[...])
    gate_i = jax.nn.sigmoid(jnp.dot(xb, wx_ref[...], preferred_element_type=F32) + bi_ref[...])
    nl = -lam_ref[...]
    softplus = jnp.maximum(nl, 0.0) + jnp.log1p(jnp.exp(-jnp.abs(nl)))
    a = jnp.exp(-LRU_C * r * softplus)
    bb = jnp.sqrt(1.0 - a * a) * (gate_i * xc)

    row = lax.broadcasted_iota(jnp.int32, (tm, 1), 0) % 8
    for k in (1, 2, 4):
        sh = -k if reverse else k
        take = (row < 8 - k) if reverse else (row >= k)
        bb = jnp.where(take, a * _shift_rows(bb, sh) + bb, bb)
        a = jnp.where(take, a * _shift_rows(a, sh), a)
    a_s[...] = a
    b_s[...] = bb
    ngroups = tm // 8

    def body(g, hprev):
        gg = ngroups - 1 - g if reverse else g
        sl = pl.ds(pl.multiple_of(gg * 8, 8), 8)
        rows = b_s[sl, :] + a_s[sl, :] * hprev
        o_ref[sl, :] = rows
        edge = rows[0:1] if reverse else rows[7:8]
        return jnp.broadcast_to(edge, (8, ch))

    carry[...] = lax.fori_loop(0, ngroups, body, carry[...])


def _lru_call(p, h0, cw, cb, wa, br, wx, bi, lam, reverse):
    b, l, _ = p.shape
    ch = wa.shape[0]
    tm = min(l, 512)
    nt = l // tm
    tile_of = (lambda i: nt - 1 - i) if reverse else (lambda i: i)
    vec = pl.BlockSpec((1, ch), lambda bi_, i: (0, 0))
    mat = pl.BlockSpec((ch, ch), lambda bi_, i: (0, 0))
    return pl.pallas_call(
        functools.partial(_lru_kernel, reverse=reverse),
        grid=(b, nt),
        in_specs=_halo_specs(tm, ch, 3, nt, tile_of) + [
            pl.BlockSpec((4, ch), lambda bi_, i: (0, 0)), vec, mat, vec, mat, vec, vec,
            pl.BlockSpec((None, 1, ch), lambda bi_, i: (bi_, 0, 0))],
        out_specs=pl.BlockSpec((None, tm, ch), lambda bi_, i: (bi_, tile_of(i), 0)),
        out_shape=jax.ShapeDtypeStruct((b, l, ch), F32),
        scratch_shapes=[pltpu.VMEM((tm, ch), F32), pltpu.VMEM((tm, ch), F32), pltpu.VMEM((8, ch), F32)],
        compiler_params=_params("parallel", "arbitrary"),
        name="rglru_bwd" if reverse else "rglru_fwd",
    )(p, p, p, cw, cb.reshape(1, ch), wa, br.reshape(1, ch), wx, bi.reshape(1, ch), lam.reshape(1, ch), h0)


def _hy_conv_kernel(prev_ref, cur_ref, next_ref, w_ref, b_ref, o_ref):
    i = pl.program_id(2)
    tm = cur_ref.shape[0]
    e = _with_halo(prev_ref, cur_ref, next_ref, i, pl.num_programs(2))
    w = w_ref[...]
    z = b_ref[...] + sum(_shift_rows(e, 1 - j) * w[j:j + 1] for j in range(3))
    o_ref[...] = z[HALO:HALO + tm]


def _hy_conv_call(p, w, bias):
    b, l, _ = p.shape
    ch = 256
    tm = min(l, 512)
    nt = l // tm
    specs = _halo_specs(tm, ch, 0, nt, nlead=3)
    def with_col(spec):
        inner = spec.index_map
        return pl.BlockSpec(spec.block_shape, lambda bi, j, i: inner(bi, j, i)[:2] + (5 + j,))
    return pl.pallas_call(
        _hy_conv_kernel,
        grid=(b, 3, nt),
        in_specs=[with_col(s) for s in specs] + [
            pl.BlockSpec((3, ch), lambda bi, j, i: (0, j)),
            pl.BlockSpec((1, ch), lambda bi, j, i: (0, j))],
        out_specs=pl.BlockSpec((None, None, tm, ch), lambda bi, j, i: (j, bi, i, 0)),
        out_shape=jax.ShapeDtypeStruct((3, b, l, ch), F32),
        compiler_params=_params("parallel", "parallel", "parallel"),
        name="hyena_short_conv",
    )(p, p, p, w, bias.reshape(1, -1))


def _hy_filter_kernel(emb_ref, w1_ref, b1_ref, fq_ref, w2_ref, b2_ref, w3_ref, dl_ref, f_ref, nrm_ref, *, seq_len):
    i = pl.program_id(0)
    tr = emb_ref.shape[0]
    emb = emb_ref[...]
    fq = fq_ref[...]
    h = jnp.sin(fq * (jnp.dot(emb, w1_ref[...], precision=HIGHEST, preferred_element_type=F32) + b1_ref[...]))
    h = jnp.sin(fq * (jnp.dot(h, w2_ref[...], precision=HIGHEST, preferred_element_type=F32) + b2_ref[...]))
    h = jnp.dot(h, w3_ref[...], precision=HIGHEST, preferred_element_type=F32)
    half = h.shape[1] // 2
    n = i * tr + lax.broadcasted_iota(jnp.int32, (tr, 1), 0)
    filt = jnp.where(n < seq_len, h[:, :half], jnp.where(n > seq_len, h[:, half:], 0.0))
    filt = filt * jnp.exp(-emb[:, 0:1] * dl_ref[...])
    f_ref[...] = filt

    @pl.when(i == 0)
    def _():
        nrm_ref[...] = jnp.zeros(nrm_ref.shape, F32)

    nrm_ref[...] += jnp.sum(jnp.abs(filt), axis=0, keepdims=True)


def _hy_filter_call(emb_circ, w1, b1, fq, w2, b2, w3, deltas2, seq_len):
    n2l = emb_circ.shape[0]
    hid = w2.shape[0]
    cols = w3.shape[1] // 2
    tr = min(n2l, 512)
    full = lambda shape: pl.BlockSpec(shape, lambda i: (0, 0))
    return pl.pallas_call(
        functools.partial(_hy_filter_kernel, seq_len=seq_len),
        grid=(n2l // tr,),
        in_specs=[pl.BlockSpec((tr, HY_EMB_PAD), lambda i: (i, 0)),
                  full((HY_EMB_PAD, hid)), full((1, hid)), full((1, hid)),
                  full((hid, hid)), full((1, hid)), full((hid, 2 * cols)), full((1, cols))],
        out_specs=[pl.BlockSpec((tr, cols), lambda i: (i, 0)), full((1, cols))],
        out_shape=[jax.ShapeDtypeStruct((n2l, cols), F32), jax.ShapeDtypeStruct((1, cols), F32)],
        compiler_params=_params("arbitrary"),
        name="hyena_filter",
    )(emb_circ, w1, b1.reshape(1, hid), fq.reshape(1, hid), w2, b2.reshape(1, hid), w3, deltas2)


def _dft_outer_kernel(f_ref, x_ref, o_ref):
    o_ref[...] = jnp.dot(f_ref[...], x_ref[...].astype(BF16), preferred_element_type=F32)


def _dft_outer_call(fmat, x):
    b, k, cols = x.shape
    m = fmat.shape[0]
    tn = min(cols, 2048)
    return pl.pallas_call(
        _dft_outer_kernel,
        grid=(b, cols // tn),
        in_specs=[pl.BlockSpec((m, k), lambda bi, j: (0, 0)),
                  pl.BlockSpec((None, k, tn), lambda bi, j: (bi, 0, j))],
        out_specs=pl.BlockSpec((None, m, tn), lambda bi, j: (bi, 0, j)),
        out_shape=jax.ShapeDtypeStruct((b, m, cols), F32),
        compiler_params=_params("parallel", "parallel"),
        name="dft_outer",
    )(fmat, x)


def _cmul(ar, ai, br, bi):
    return ar * br - ai * bi, ar * bi + ai * br


def _inner_dft(fblk, re, im):
    n2 = re.shape[0]
    z = jnp.dot(fblk, jnp.concatenate([re, im], axis=0).astype(BF16), preferred_element_type=F32)
    return z[:n2], z[n2:]


def _tile_lanes(t, width):
    return jnp.concatenate([t] * (width // t.shape[1]), axis=1)


def _hy_spec_kernel(a_ref, tw_ref, fblk_ref, nrm_ref, h_ref):
    cols = a_ref.shape[-1]
    ar, ai = _cmul(a_ref[0], a_ref[1], _tile_lanes(tw_ref[0], cols), _tile_lanes(tw_ref[1], cols))
    zr, zi = _inner_dft(fblk_ref[...], ar, ai)
    inv = 1.0 / nrm_ref[...]
    h_ref[0] = zr * inv
    h_ref[1] = zi * inv


def _hy_spec_call(a, twb, fblk, nrm):
    _, n1, n2, cols = a.shape
    return pl.pallas_call(
        _hy_spec_kernel,
        grid=(n1,),
        in_specs=[pl.BlockSpec((2, None, n2, cols), lambda k: (0, k, 0, 0)),
                  pl.BlockSpec((2, None, n2, 128), lambda k: (0, k, 0, 0)),
                  pl.BlockSpec((2 * n2, 2 * n2), lambda k: (0, 0)),
                  pl.BlockSpec((1, cols), lambda k: (0, 0))],
        out_specs=pl.BlockSpec((None, 2, n2, cols), lambda k: (k, 0, 0, 0)),
        out_shape=jax.ShapeDtypeStruct((n1, 2, n2, cols), F32),
        compiler_params=_params("parallel"),
        name="hyena_filter_spectrum",
    )(a, twb, fblk, nrm)


def _hy_mid_kernel(a_ref, tw_ref, fblk_ref, fblkt_ref, h_ref, o_ref):
    ch = a_ref.shape[-1]
    twr = _tile_lanes(tw_ref[0], ch)
    twi = _tile_lanes(tw_ref[1], ch)
    ar, ai = _cmul(a_ref[0], a_ref[1], twr, twi)
    zr, zi = _inner_dft(fblk_ref[...], ar, ai)
    yr, yi = _cmul(zr, zi, h_ref[0], h_ref[1])
    br, bi = _inner_dft(fblkt_ref[...], yr, yi)
    br, bi = _cmul(br, bi, twr, -twi)
    o_ref[0] = br
    o_ref[1] = bi


def _hy_mid_call(a, twb, fblk, fblkt, spec, order):
    b, _, n1, n2, ch = a.shape
    return pl.pallas_call(
        _hy_mid_kernel,
        grid=(b, n1),
        in_specs=[pl.BlockSpec((None, 2, None, n2, ch), lambda bi, k: (bi, 0, k, 0, 0)),
                  pl.BlockSpec((2, None, n2, 128), lambda bi, k: (0, k, 0, 0)),
                  pl.BlockSpec((2 * n2, 2 * n2), lambda bi, k: (0, 0)),
                  pl.BlockSpec((2 * n2, 2 * n2), lambda bi, k: (0, 0)),
                  pl.BlockSpec((None, 2, n2, ch), lambda bi, k: (k, 0, 0, order))],
        out_specs=pl.BlockSpec((None, 2, None, n2, ch), lambda bi, k: (bi, 0, k, 0, 0)),
        out_shape=jax.ShapeDtypeStruct(a.shape, F32),
        compiler_params=_params("parallel", "parallel"),
        name="hyena_spectral_mid",
    )(a, twb, fblk, fblkt, spec)


def _hy_out_kernel(f_ref, b_ref, u_ref, g_ref, sk_ref, o_ref):
    y = jnp.dot(f_ref[...], b_ref[...].astype(BF16), preferred_element_type=F32)
    o_ref[...] = g_ref[...] * (y + sk_ref[...] * u_ref[...])


def _hy_out_call(finv, bst, u, gate, skip_t):
    b, k, cols = bst.shape
    m = finv.shape[0]
    tn = min(cols, 2048)
    return pl.pallas_call(
        _hy_out_kernel,
        grid=(b, cols // tn),
        in_specs=[pl.BlockSpec((m, k), lambda bi, j: (0, 0)),
                  pl.BlockSpec((None, k, tn), lambda bi, j: (bi, 0, j)),
                  pl.BlockSpec((None, m, tn), lambda bi, j: (bi, 0, j)),
                  pl.BlockSpec((None, m, tn), lambda bi, j: (bi, 0, j)),
                  pl.BlockSpec((1, tn), lambda bi, j: (0, 0))],
        out_specs=pl.BlockSpec((None, m, tn), lambda bi, j: (bi, 0, j)),
        out_shape=jax.ShapeDtypeStruct((b, m, cols), F32),
        compiler_params=_params("parallel", "parallel"),
        name="hyena_dft_out",
    )(finv, bst, u, gate, skip_t)


def _hy_small_kernel(z_ref, filt_ref, nrm_ref, fs_ref, finv_ref, sk_ref, o_ref):
    l = z_ref.shape[1]
    ch = z_ref.shape[2]
    fs = fs_ref[...]
    n = fs.shape[0] // 2
    spec = jnp.dot(fs, filt_ref[...], precision=HIGHEST, preferred_element_type=F32) / nrm_ref[...]
    sig = z_ref[2]
    for order in range(2):
        u = jnp.dot(fs[:, :l], sig, precision=HIGHEST, preferred_element_type=F32)
        hr = spec[:n, order * ch:(order + 1) * ch]
        hi = spec[n:, order * ch:(order + 1) * ch]
        yr, yi = _cmul(u[:n], u[n:], hr, hi)
        y = jnp.dot(finv_ref[...], jnp.concatenate([yr, yi], axis=0), precision=HIGHEST, preferred_element_type=F32)
        sig = z_ref[order] * (y + sk_ref[order:order + 1] * sig)
    o_ref[...] = sig


def _hy_small_call(z, filt, nrm, fs, finv, skip):
    _, b, l, ch = z.shape
    full = lambda a: pl.BlockSpec(a.shape, lambda bi: (0,) * a.ndim)
    return pl.pallas_call(
        _hy_small_kernel,
        grid=(b,),
        in_specs=[pl.BlockSpec((3, None, l, ch), lambda bi: (0, bi, 0, 0)),
                  full(filt), full(nrm), full(fs), full(finv), full(skip)],
        out_specs=pl.BlockSpec((None, l, ch), lambda bi: (bi, 0, 0)),
        out_shape=jax.ShapeDtypeStruct((b, l, ch), F32),
        compiler_params=_params("parallel"),
        name="hyena_short_sequence",
    )(z, filt, nrm, fs, finv, skip)


def _post_kernel(x_ref, ya_ref, yb_ref, hf_ref, hb_ref, gt_ref, yd_ref, mod_ref, wo_ref, g1_ref, g2_ref, g3_ref,
                 w1_ref, w2_ref, o_ref):
    m = mod_ref[...]
    gt = gt_ref[...]
    gelu = 0.5 * gt * (1.0 + jnp.tanh(math.sqrt(2.0 / math.pi) * (gt + 0.044715 * gt * gt * gt)))
    yc = (hf_ref[...] + hb_ref[...]) * gelu
    cat = jnp.concatenate([ya_ref[...].astype(BF16), yb_ref[...], yc.astype(BF16), yd_ref[...].astype(BF16)], axis=-1)
    mix = jnp.dot(cat, wo_ref[...], preferred_element_type=F32)
    x1 = x_ref[...] + m[2:3] * _rms(mix, g1_ref[...])
    h = (_rms(x1, g2_ref[...]) * (1.0 + m[4:5]) + m[3:4]).astype(BF16)
    dff = w1_ref.shape[1]
    fc = 1024
    acc = jnp.zeros(x1.shape, F32)
    for c in range(dff // fc):
        a = jnp.maximum(jnp.dot(h, w1_ref[:, c * fc:(c + 1) * fc], preferred_element_type=F32), 0.0)
        acc = acc + jnp.dot((a * a).astype(BF16), w2_ref[c * fc:(c + 1) * fc, :], preferred_element_type=F32)
    o_ref[...] = x1 + m[5:6] * _rms(acc, g3_ref[...])


def _post_call(x, ya, yb, hf, hb, p, yd, mod, wo, g1, g2, g3, w1, w2):
    b, l, d = x.shape
    ch = ya.shape[-1]
    tm = min(l, 512)
    row = lambda w, col=0: pl.BlockSpec((None, tm, w), lambda bi, i: (bi, i, col))
    vec = pl.BlockSpec((1, d), lambda bi, i: (0, 0))
    return pl.pallas_call(
        _post_kernel,
        grid=(b, l // tm),
        in_specs=[row(d), row(ch), row(ch), row(ch), row(ch), row(ch, 4), row(ch),
                  pl.BlockSpec((None, 6, d), lambda bi, i: (bi, 0, 0)),
                  _resident(wo.shape, lambda bi, i: (0, 0)), vec, vec, vec,
                  _resident(w1.shape, lambda bi, i: (0, 0)),
                  _resident(w2.shape, lambda bi, i: (0, 0))],
        out_specs=row(d),
        out_shape=jax.ShapeDtypeStruct((b, l, d), F32),
        compiler_params=_params("parallel", "parallel"),
        name="out_proj_mlp",
    )(x, ya, yb, hf, hb, p, yd, mod, wo, g1.reshape(1, d), g2.reshape(1, d), g3.reshape(1, d), w1, w2)


def _rope_tables(seq_len):
    t = jnp.arange(seq_len)
    inv = ROPE_THETA ** (-jnp.arange(ROPE_FREQS, dtype=F32) / ROPE_FREQS)
    ang_r = (t // GRID_W).astype(F32)[:, None] * inv
    ang_c = (t % GRID_W).astype(F32)[:, None] * inv
    cos = jnp.concatenate([jnp.cos(ang_r)] * 2 + [jnp.cos(ang_c)] * 2, axis=1)
    sin = jnp.concatenate([-jnp.sin(ang_r), jnp.sin(ang_r), -jnp.sin(ang_c), jnp.sin(ang_c)], axis=1)
    return jnp.tile(cos, (1, 2)), jnp.tile(sin, (1, 2))


def _hyena_embedding(seq_len):
    t01 = jnp.linspace(0.0, 1.0, seq_len, dtype=F32)[:, None]
    w = 2.0 * math.pi * jnp.arange(seq_len, dtype=F32) / seq_len
    bands = jnp.linspace(1e-4, HY_BANDS - 1, HY_BANDS, dtype=F32)
    z = w[:, None] * bands[None, :]
    emb = jnp.concatenate([t01, jnp.cos(z), -jnp.sin(z)], axis=-1)
    circ = jnp.concatenate([emb, emb[:1], emb[:0:-1]], axis=0)
    return jnp.pad(circ, ((0, 0), (0, HY_EMB_PAD - HY_EMB)))


def _dft_cos_sin(rows, cols, n):
    ang = (2.0 * math.pi / n) * ((jnp.arange(rows)[:, None] * jnp.arange(cols)[None, :]) % n).astype(F32)
    return jnp.cos(ang), -jnp.sin(ang)


def _dft_tables(seq_len):
    n = 2 * seq_len
    n2 = DFT_N2
    n1 = n // n2
    fr, fi = _dft_cos_sin(n1, n1, n1)
    f_outer = jnp.concatenate([fr, fi], axis=0).astype(BF16)
    f_inv = (jnp.concatenate([fr, fi], axis=1)[:n1 // 2] * (1.0 / n)).astype(BF16)
    gr, gi = _dft_cos_sin(n2, n2, n2)
    fblk = jnp.concatenate([jnp.concatenate([gr, -gi], axis=1),
                            jnp.concatenate([gi, gr], axis=1)], axis=0).astype(BF16)
    tr, ti = _dft_cos_sin(n1, n2, n)
    twb = jnp.broadcast_to(jnp.stack([tr, ti])[..., None], (2, n1, n2, 128))
    return f_outer, f_inv, fblk, fblk.T, twb


def _dense_dft_tables(seq_len):
    n = 2 * seq_len
    fr, fi = _dft_cos_sin(n, n, n)
    fs = jnp.concatenate([fr, fi], axis=0)
    finv = jnp.concatenate([fr, fi], axis=1)[:seq_len] * (1.0 / n)
    return fs, finv


def _block_diag(w):
    g, c, _ = w.shape
    eye = jnp.eye(g, dtype=w.dtype)
    return (eye[:, None, :, None] * w[:, :, None, :]).reshape(g * c, g * c)


def _hyena_long(p, l_w, tables, filt, nrm):
    b, l, _ = p.shape
    f_outer, f_inv, fblk, fblkt, twb = tables
    n2 = DFT_N2
    n1 = 2 * l // n2
    ch = 256
    z = _hy_conv_call(p, l_w["hy_conv_w"], l_w["hy_conv_b"])
    fa = _dft_outer_call(f_outer, filt.reshape(1, n1, n2 * 2 * ch))
    spec = _hy_spec_call(fa.reshape(2, n1, n2, 2 * ch), twb, fblk, nrm)
    zf = z.reshape(3, b, n1 // 2, n2 * ch)
    skip_t = jnp.tile(l_w["hy_skip"], (1, min(n2, 2048 // ch)))
    sig = zf[2]
    for order in range(2):
        a = _dft_outer_call(f_outer[:, :n1 // 2], sig)
        bm = _hy_mid_call(a.reshape(b, 2, n1, n2, ch), twb, fblk, fblkt, spec, order)
        sig = _hy_out_call(f_inv, bm.reshape(b, 2 * n1, n2 * ch), sig, zf[order], skip_t[order:order + 1])
    return sig.reshape(b, l, ch)


def _layer_weights(args, l):
    return {k: v[l] for k, v in args.items()}


def kernel(x, c, ctx, c_ctx, w_mod, b_mod, g_pre_mix, g_post_mix, g_pre_mlp, g_post_mlp, w_in, w_out, pool_w, pool_scale, q_norm_g, k_norm_g, lru_conv_w, lru_conv_b, lru_wa, lru_ba, lru_wx, lru_bx, lru_lambda, hy_conv_w, hy_conv_b, hy_w1, hy_b1, hy_freq, hy_w2, hy_b2, hy_w3, hy_skip, mlp_w1, mlp_w2):
    b, l, d = x.shape
    lc = ctx.shape[1]
    depth = w_mod.shape[0]
    per_layer = dict(g_pre_mix=g_pre_mix, g_post_mix=g_post_mix, g_pre_mlp=g_pre_mlp, g_post_mlp=g_post_mlp,
                     w_in=w_in.astype(BF16), w_out=w_out.astype(BF16), pool_w=pool_w, pool_scale=pool_scale,
                     q_norm_g=q_norm_g, k_norm_g=k_norm_g, lru_conv_w=lru_conv_w, lru_conv_b=lru_conv_b,
                     lru_wa=lru_wa, lru_ba=lru_ba, lru_wx=lru_wx, lru_bx=lru_bx, lru_lambda=lru_lambda,
                     hy_conv_w=hy_conv_w, hy_conv_b=hy_conv_b, hy_w1=hy_w1, hy_b1=hy_b1, hy_freq=hy_freq,
                     hy_w2=hy_w2, hy_b2=hy_b2, hy_w3=hy_w3, hy_skip=hy_skip,
                     mlp_w1=mlp_w1.astype(BF16), mlp_w2=mlp_w2.astype(BF16))

    cos, sin = _rope_tables(l)
    avg = _block_diag(jnp.full((2, HEAD_DIM, HEAD_DIM), 1.0 / HEAD_DIM, F32)).astype(BF16)
    emb_x, emb_c = _hyena_embedding(l), _hyena_embedding(lc)
    dft_x = _dft_tables(l)
    fs_c, finv_c = _dense_dft_tables(lc)
    deltas = jnp.abs(jnp.linspace(HY_MIN_DECAY, HY_MAX_DECAY, 256, dtype=F32))
    deltas2 = jnp.tile(deltas, 2).reshape(1, 512)

    cond = jnp.zeros((8, d), F32).at[:b].set(c).at[b].set(c_ctx)
    mods = _mod_call(cond, w_mod, b_mod).reshape(depth, 8, 6, d)

    cx = ctx
    for li in range(depth):
        w = _layer_weights(per_layer, li)
        need_ctx = li < depth - 1
        mod_x = mods[li, :b]
        mod_c = jnp.broadcast_to(mods[li, b:b + 1], (b, 6, d))
        w1p = jnp.pad(w["hy_w1"], ((0, HY_EMB_PAD - HY_EMB), (0, 0)))
        pool_bd = _block_diag(w["pool_w"]).astype(BF16)
        gq = jnp.tile(w["q_norm_g"], 2).reshape(1, 128)
        gk = jnp.tile(w["k_norm_g"], 2).reshape(1, 128)
        wa = [_block_diag(w["lru_wa"][dr]).astype(BF16) for dr in range(2)]
        wx = [_block_diag(w["lru_wx"][dr]).astype(BF16) for dr in range(2)]

        px = _inproj_call(x, mod_x, w["g_pre_mix"], w["w_in"])
        pc = _inproj_call(cx, mod_c, w["g_pre_mix"], w["w_in"])

        qx, ktx, vx = _qkv_call(px, cos, sin, gq, gk, avg)
        qc, ktc, vc = _qkv_call(pc, None, None, gq, gk, avg)
        yb_x = _flash_call(qx, ktx, vx, ktc, vc)

        hs_c, hs_x = [], []
        for dr in range(2):
            lw = (w["lru_conv_w"], w["lru_conv_b"], wa[dr], w["lru_ba"][dr], wx[dr], w["lru_bx"][dr],
                  w["lru_lambda"][dr])
            hc = _lru_call(pc, jnp.zeros((b, 1, 256), F32), *lw, reverse=bool(dr))
            h0 = hc[:, :1] if dr else hc[:, -1:]
            hs_c.append(hc)
            hs_x.append(_lru_call(px, h0, *lw, reverse=bool(dr)))

        ya_x = _pool_call(px, pool_bd, w["pool_scale"])
        hy = (w1p, w["hy_b1"], w["hy_freq"], w["hy_w2"], w["hy_b2"], w["hy_w3"], deltas2)
        filt_x, nrm_x = _hy_filter_call(emb_x, *hy, seq_len=l)
        yd_x = _hyena_long(px, w, dft_x, filt_x, nrm_x)

        x = _post_call(x, ya_x, yb_x, hs_x[0], hs_x[1], px, yd_x, mod_x, w["w_out"], w["g_post_mix"],
                       w["g_pre_mlp"], w["g_post_mlp"], w["mlp_w1"], w["mlp_w2"])

        if need_ctx:
            yb_c = _flash_call(qc, ktc, vc)
            ya_c = _pool_call(pc, pool_bd, w["pool_scale"])
            filt_c, nrm_c = _hy_filter_call(emb_c, *hy, seq_len=lc)
            z_c = _hy_conv_call(pc, w["hy_conv_w"], w["hy_conv_b"])
            yd_c = _hy_small_call(z_c, filt_c, nrm_c, fs_c, finv_c, w["hy_skip"])
            cx = _post_call(cx, ya_c, yb_c, hs_c[0], hs_c[1], pc, yd_c, mod_c, w["w_out"], w["g_post_mix"],
                            w["g_pre_mlp"], w["g_post_mlp"], w["mlp_w1"], w["mlp_w2"])
    return x
```

```python
import functools
import math

import jax
import jax.numpy as jnp
from jax import lax
from jax.experimental import pallas as pl
from jax.experimental.pallas import tpu as pltpu

F32 = jnp.float32
BF16 = jnp.bfloat16
EPS = 1e-6

GRID_W = 64
HEAD_DIM = 64
N_KV_HEADS = 2
ROPE_FREQS = 16
ROPE_THETA = 10000.0
POOL_WINDOWS = (2, 4, 8, 16)
LRU_C = 8.0
HY_EMB = 33
HY_EMB_PAD = 64
HY_BANDS = 16
HY_MIN_DECAY = math.log(1e-2) / 1.5
HY_MAX_DECAY = math.log(1e-2) / 0.3
DFT_N2 = 128
HALO = 16
QUERY_CHAIN = 256
VMEM_LIMIT = 56 * 1024 * 1024
HIGHEST = lax.Precision.HIGHEST


def _params(*sem):
    return pltpu.CompilerParams(dimension_semantics=sem, vmem_limit_bytes=VMEM_LIMIT)


def _resident(shape, index_map):
    return pl.BlockSpec(shape, index_map, pipeline_mode=pl.Buffered(1))


def _rms(x, g):
    return x * lax.rsqrt(jnp.mean(x * x, axis=-1, keepdims=True) + EPS) * g


def _shift_rows(x, k):
    n = x.shape[0]
    return pltpu.roll(x, k % n, 0)


def _with_halo(prev_ref, cur_ref, next_ref, tile, ntiles):
    prev = jnp.where(tile > 0, prev_ref[...], 0.0)
    nxt = jnp.where(tile < ntiles - 1, next_ref[...], 0.0)
    return jnp.concatenate([prev, cur_ref[...], nxt], axis=0)


def _halo_specs(tm, width, col, ntiles, tile_of=lambda i: i, nlead=2):
    r = tm // HALO
    last = ntiles * r - 1

    def prev_map(*g):
        return (g[0], jnp.maximum(tile_of(g[nlead - 1]) * r - 1, 0), col)

    def cur_map(*g):
        return (g[0], tile_of(g[nlead - 1]), col)

    def next_map(*g):
        return (g[0], jnp.minimum((tile_of(g[nlead - 1]) + 1) * r, last), col)

    return [pl.BlockSpec((None, HALO, width), prev_map),
            pl.BlockSpec((None, tm, width), cur_map),
            pl.BlockSpec((None, HALO, width), next_map)]


def _mod_kernel(c_ref, w_ref, b_ref, o_ref):
    cond = c_ref[...]
    act = (cond * jax.nn.sigmoid(cond)).astype(BF16)
    o_ref[...] = jnp.dot(act, w_ref[...].astype(BF16), preferred_element_type=F32) + b_ref[...]


def _mod_call(cond, w_mod, b_mod):
    depth, d, n = w_mod.shape
    tn = 1024
    return pl.pallas_call(
        _mod_kernel,
        grid=(depth, n // tn),
        in_specs=[pl.BlockSpec((8, d), lambda l, j: (0, 0)),
                  pl.BlockSpec((None, d, tn), lambda l, j: (l, 0, j)),
                  pl.BlockSpec((None, 1, tn), lambda l, j: (l, 0, j))],
        out_specs=pl.BlockSpec((None, 8, tn), lambda l, j: (l, 0, j)),
        out_shape=jax.ShapeDtypeStruct((depth, 8, n), F32),
        compiler_params=_params("parallel", "parallel"),
        name="adaln_mod",
    )(cond, w_mod, b_mod.reshape(depth, 1, n))


def _inproj_kernel(x_ref, mod_ref, g_ref, w_ref, o_ref):
    m = mod_ref[...]
    h = _rms(x_ref[...], g_ref[...]) * (1.0 + m[1:2]) + m[0:1]
    o_ref[...] = jnp.dot(h.astype(BF16), w_ref[...], preferred_element_type=F32)


def _inproj_call(x, mod, g, w):
    b, l, d = x.shape
    n = w.shape[1]
    tm = min(l, 512)
    return pl.pallas_call(
        _inproj_kernel,
        grid=(b, l // tm),
        in_specs=[pl.BlockSpec((None, tm, d), lambda bi, i: (bi, i, 0)),
                  pl.BlockSpec((None, 6, d), lambda bi, i: (bi, 0, 0)),
                  pl.BlockSpec((1, d), lambda bi, i: (0, 0)),
                  _resident((d, n), lambda bi, i: (0, 0))],
        out_specs=pl.BlockSpec((None, tm, n), lambda bi, i: (bi, i, 0)),
        out_shape=jax.ShapeDtypeStruct((b, l, n), F32),
        compiler_params=_params("parallel", "parallel"),
        name="in_proj",
    )(x, mod, g.reshape(1, d), w)


def _pool_kernel(prev_ref, cur_ref, next_ref, w_ref, sc_ref, o_ref, *, seq_len):
    i = pl.program_id(1)
    tm, ch = cur_ref.shape
    e = _with_halo(prev_ref, cur_ref, next_ref, i, pl.num_programs(1))
    s2 = e + _shift_rows(e, 1)
    s4 = _shift_rows(s2, 1) + _shift_rows(s2, -1)
    s8 = _shift_rows(s4, 2) + _shift_rows(s4, -2)
    s16 = _shift_rows(s8, 4) + _shift_rows(s8, -4)
    t = i * tm + lax.broadcasted_iota(jnp.int32, (tm, 1), 0)
    group = lax.broadcasted_iota(jnp.int32, (1, ch), 1) // (ch // len(POOL_WINDOWS))
    mean = jnp.zeros((tm, ch), F32)
    for gi, (win, s) in enumerate(zip(POOL_WINDOWS, (s2, s4, s8, s16))):
        lo = jnp.maximum(t - win // 2, 0)
        hi = jnp.minimum(t + win - win // 2, seq_len)
        m = s[HALO:HALO + tm] / (hi - lo).astype(F32)
        mean = jnp.where(group == gi, m, mean)
    d = (mean - cur_ref[...]).astype(BF16)
    o_ref[...] = jnp.dot(d, w_ref[...], preferred_element_type=F32) * sc_ref[...]


def _pool_call(p, w_bd, scale):
    b, l, _ = p.shape
    ch = w_bd.shape[0]
    tm = min(l, 512)
    nt = l // tm
    return pl.pallas_call(
        functools.partial(_pool_kernel, seq_len=l),
        grid=(b, nt),
        in_specs=_halo_specs(tm, ch, 0, nt) + [
            pl.BlockSpec((ch, ch), lambda bi, i: (0, 0)),
            pl.BlockSpec((1, ch), lambda bi, i: (0, 0))],
        out_specs=pl.BlockSpec((None, tm, ch), lambda bi, i: (bi, i, 0)),
        out_shape=jax.ShapeDtypeStruct((b, l, ch), F32),
        compiler_params=_params("parallel", "parallel"),
        name="pool_mixer",
    )(p, p, p, w_bd, scale.reshape(1, ch))


def _head_norm(x, g, avg):
    sq = x * x
    hi = sq.astype(BF16)
    lo = (sq - hi.astype(F32)).astype(BF16)
    ms = (jnp.dot(hi, avg, preferred_element_type=F32) + jnp.dot(lo, avg, preferred_element_type=F32))
    return x * lax.rsqrt(ms + EPS) * g


def _rope(x, cos, sin):
    n = x.shape[1]
    lane = lax.broadcasted_iota(jnp.int32, (1, n), 1)
    partner = jnp.where(lane % 32 < 16, pltpu.roll(x, n - 16, 1), pltpu.roll(x, 16, 1))
    return x * cos + partner * sin


def _qkv_kernel(*refs, rope):
    if rope:
        q_ref, kv_ref, cos_ref, sin_ref, gq_ref, gk_ref, avg_ref, qt_ref, ko_ref, vt_ref = refs
    else:
        q_ref, kv_ref, gq_ref, gk_ref, avg_ref, qt_ref, ko_ref, vt_ref = refs
    avg = avg_ref[...]
    for half in range(2):
        q = _head_norm(q_ref[:, half * 128:(half + 1) * 128], gq_ref[...], avg)
        if rope:
            q = _rope(q, cos_ref[...], sin_ref[...])
        qt = (q * (HEAD_DIM ** -0.5)).T
        for h in range(2):
            qt_ref[2 * half + h] = qt[h * HEAD_DIM:(h + 1) * HEAD_DIM].astype(BF16)
    k = _head_norm(kv_ref[:, 0:128], gk_ref[...], avg)
    if rope:
        k = _rope(k, cos_ref[...], sin_ref[...])
    ko_ref[...] = k.astype(BF16)
    vt = kv_ref[:, 128:256].T
    for h in range(N_KV_HEADS):
        vt_ref[h] = vt[h * HEAD_DIM:(h + 1) * HEAD_DIM].astype(BF16)


def _qkv_call(p, cos, sin, gq, gk, avg):
    b, l, _ = p.shape
    tm = min(l, 512)
    rope = cos is not None
    tables = [cos, sin] if rope else []
    table_specs = [pl.BlockSpec((tm, 128), lambda bi, i: (i, 0))] * 2 if rope else []
    const = pl.BlockSpec((1, 128), lambda bi, i: (0, 0))
    return pl.pallas_call(
        functools.partial(_qkv_kernel, rope=rope),
        grid=(b, l // tm),
        in_specs=[pl.BlockSpec((None, tm, 256), lambda bi, i: (bi, i, 1)),
                  pl.BlockSpec((None, tm, 256), lambda bi, i: (bi, i, 2))] + table_specs + [
                  const, const, pl.BlockSpec((128, 128), lambda bi, i: (0, 0))],
        out_specs=[pl.BlockSpec((None, 2 * N_KV_HEADS, HEAD_DIM, tm), lambda bi, i: (bi, 0, 0, i)),
                   pl.BlockSpec((None, tm, 128), lambda bi, i: (bi, i, 0)),
                   pl.BlockSpec((None, N_KV_HEADS, None, HEAD_DIM, tm), lambda bi, i: (bi, 0, i, 0, 0))],
        out_shape=[jax.ShapeDtypeStruct((b, 2 * N_KV_HEADS, HEAD_DIM, l), BF16),
                   jax.ShapeDtypeStruct((b, l, 128), BF16),
                   jax.ShapeDtypeStruct((b, N_KV_HEADS, l // tm, HEAD_DIM, tm), BF16)],
        compiler_params=_params("parallel", "parallel"),
        name="qkv_prep",
    )(p, p, *tables, gq, gk, avg)


def _flash_kernel(*refs, with_ctx):
    if with_ctx:
        q_ref, k_ref, vt_ref, kc_ref, vtc_ref, o_ref, qs, m_s, l_s, acc, sb0, sb1 = refs
    else:
        q_ref, k_ref, vt_ref, o_ref, qs, m_s, l_s, acc, sb0, sb1 = refs
    h = pl.program_id(1)
    kj = pl.program_id(3)
    tq = q_ref.shape[-1]
    qc = QUERY_CHAIN
    per_head = tq // qc
    nchain = 2 * per_head
    nkc, _, tkc = vt_ref.shape
    nunits = nchain * nkc

    def sublane_all(op, x):
        for sh in (1, 2, 4):
            x = op(x, pltpu.roll(x, sh, 0))
        return x

    def softmax_pv(s, vt, c):
        s3 = s.reshape(s.shape[0] // 8, 8, qc)
        m_prev = m_s[c]
        m_new = jnp.maximum(m_prev, sublane_all(jnp.maximum, jnp.max(s3, axis=0)))
        alpha = jnp.exp(m_prev - m_new)
        p3 = jnp.exp(s3 - m_new[None])
        l_s[c] = alpha * l_s[c] + jnp.sum(p3, axis=0)
        pv = jnp.dot(vt, p3.reshape(s.shape).astype(BF16), preferred_element_type=F32)
        acc[c] = (acc[c].reshape(HEAD_DIM // 8, 8, qc) * alpha[None]).reshape(HEAD_DIM, qc) + pv
        m_s[c] = m_new

    def scores(u, buf):
        rows = pl.ds(pl.multiple_of((u // nchain) * tkc, tkc), tkc)
        buf[...] = jnp.dot(k_ref[rows, :], qs[u % nchain], preferred_element_type=F32)

    def consume(u, buf):
        softmax_pv(buf[...], vt_ref[u // nchain], u % nchain)

    @pl.when(kj == 0)
    def _():
        qs[...] = jnp.zeros(qs.shape, BF16)
        rows = pl.ds(pl.multiple_of(h * HEAD_DIM, HEAD_DIM), HEAD_DIM)
        for c in range(nchain):
            qs[c, rows, :] = q_ref[c // per_head, :, (c % per_head) * qc:(c % per_head + 1) * qc]
        m_s[...] = jnp.full(m_s.shape, -1e30, F32)
        l_s[...] = jnp.zeros(l_s.shape, F32)
        acc[...] = jnp.zeros(acc.shape, F32)
        if with_ctx:
            for c in range(nchain):
                softmax_pv(jnp.dot(kc_ref[...], qs[c], preferred_element_type=F32), vtc_ref[...], c)

    scores(0, sb0)

    def pair(j, carry):
        scores(2 * j + 1, sb1)
        consume(2 * j, sb0)
        scores(2 * j + 2, sb0)
        consume(2 * j + 1, sb1)
        return carry

    lax.fori_loop(0, nunits // 2 - 1, pair, 0)
    scores(nunits - 1, sb1)
    consume(nunits - 2, sb0)
    consume(nunits - 1, sb1)

    @pl.when(kj == pl.num_programs(3) - 1)
    def _():
        for r in range(per_head):
            halves = []
            for c in (r, r + per_head):
                l = sublane_all(jnp.add, l_s[c])
                halves.append((acc[c].reshape(HEAD_DIM // 8, 8, qc) / l[None]).reshape(HEAD_DIM, qc))
            o_ref[r * qc:(r + 1) * qc, :] = jnp.concatenate(halves, axis=0).T.astype(o_ref.dtype)


def _flash_call(qt, k, vt, kc=None, vtc=None):
    b, _, _, lq = qt.shape
    lk = k.shape[1]
    tkc = vt.shape[-1]
    tq = min(lq, 512)
    tk = lk
    nkc = tk // tkc
    nchain = 2 * tq // QUERY_CHAIN
    with_ctx = kc is not None
    in_specs = [pl.BlockSpec((None, 2, HEAD_DIM, tq), lambda bi, h, i, j: (bi, h, 0, i)),
                pl.BlockSpec((None, tk, 128), lambda bi, h, i, j: (bi, j, 0)),
                pl.BlockSpec((None, None, nkc, HEAD_DIM, tkc), lambda bi, h, i, j: (bi, h, j, 0, 0))]
    args = [qt, k, vt]
    if with_ctx:
        c = kc.shape[1]
        in_specs += [pl.BlockSpec((None, c, 128), lambda bi, h, i, j: (bi, 0, 0)),
                     pl.BlockSpec((None, None, None, HEAD_DIM, c), lambda bi, h, i, j: (bi, h, 0, 0, 0))]
        args += [kc, vtc]
    return pl.pallas_call(
        functools.partial(_flash_kernel, with_ctx=with_ctx),
        grid=(b, N_KV_HEADS, lq // tq, lk // tk),
        in_specs=in_specs,
        out_specs=pl.BlockSpec((None, tq, 128), lambda bi, h, i, j: (bi, i, h)),
        out_shape=jax.ShapeDtypeStruct((b, lq, 256), BF16),
        scratch_shapes=[pltpu.VMEM((nchain, 2 * HEAD_DIM, QUERY_CHAIN), BF16),
                        pltpu.VMEM((nchain, 8, QUERY_CHAIN), F32),
                        pltpu.VMEM((nchain, 8, QUERY_CHAIN), F32),
                        pltpu.VMEM((nchain, HEAD_DIM, QUERY_CHAIN), F32),
                        pltpu.VMEM((tkc, QUERY_CHAIN), F32),
                        pltpu.VMEM((tkc, QUERY_CHAIN), F32)],
        compiler_params=_params("parallel", "parallel", "parallel", "arbitrary"),
        name="flash_attention",
    )(*args)


def _lru_kernel(prev_ref, cur_ref, next_ref, cw_ref, cb_ref, wa_ref, br_ref, wx_ref, bi_ref, lam_ref, h0_ref,
                o_ref, a_s, b_s, carry, *, reverse):
    i = pl.program_id(1)
    nt = pl.num_programs(1)
    tm, ch = cur_ref.shape
    tile = nt - 1 - i if reverse else i

    @pl.when(i == 0)
    def _():
        carry[...] = jnp.broadcast_to(h0_ref[...], carry.shape)

    e = _with_halo(prev_ref, cur_ref, next_ref, tile, nt)
    cw = cw_ref[...]
    xc = cb_ref[...] + sum(_shift_rows(e, 2 - j) * cw[j:j + 1] for j in range(4))
    xc = xc[HALO:HALO + tm]
    xb = xc.astype(BF16)
    r = jax.nn.sigmoid(jnp.dot(xb, wa_ref[...], preferred_element_type=F32) + br_ref[...])
    gate_i = jax.nn.sigmoid(jnp.dot(xb, wx_ref[...], preferred_element_type=F32) + bi_ref[...])
    nl = -lam_ref[...]
    softplus = jnp.maximum(nl, 0.0) + jnp.log1p(jnp.exp(-jnp.abs(nl)))
    a = jnp.exp(-LRU_C * r * softplus)
    bb = jnp.sqrt(1.0 - a * a) * (gate_i * xc)

    row = lax.broadcasted_iota(jnp.int32, (tm, 1), 0) % 8
    for k in (1, 2, 4):
        sh = -k if reverse else k
        take = (row < 8 - k) if reverse else (row >= k)
        bb = jnp.where(take, a * _shift_rows(bb, sh) + bb, bb)
        a = jnp.where(take, a * _shift_rows(a, sh), a)
    a_s[...] = a
    b_s[...] = bb
    ngroups = tm // 8

    def body(g, hprev):
        gg = ngroups - 1 - g if reverse else g
        sl = pl.ds(pl.multiple_of(gg * 8, 8), 8)
        rows = b_s[sl, :] + a_s[sl, :] * hprev
        o_ref[sl, :] = rows
        edge = rows[0:1] if reverse else rows[7:8]
        return jnp.broadcast_to(edge, (8, ch))

    carry[...] = lax.fori_loop(0, ngroups, body, carry[...])


def _lru_call(p, h0, cw, cb, wa, br, wx, bi, lam, reverse):
    b, l, _ = p.shape
    ch = wa.shape[0]
    tm = min(l, 512)
    nt = l // tm
    tile_of = (lambda i: nt - 1 - i) if reverse else (lambda i: i)
    vec = pl.BlockSpec((1, ch), lambda bi_, i: (0, 0))
    mat = pl.BlockSpec((ch, ch), lambda bi_, i: (0, 0))
    return pl.pallas_call(
        functools.partial(_lru_kernel, reverse=reverse),
        grid=(b, nt),
        in_specs=_halo_specs(tm, ch, 3, nt, tile_of) + [
            pl.BlockSpec((4, ch), lambda bi_, i: (0, 0)), vec, mat, vec, mat, vec, vec,
            pl.BlockSpec((None, 1, ch), lambda bi_, i: (bi_, 0, 0))],
        out_specs=pl.BlockSpec((None, tm, ch), lambda bi_, i: (bi_, tile_of(i), 0)),
        out_shape=jax.ShapeDtypeStruct((b, l, ch), F32),
        scratch_shapes=[pltpu.VMEM((tm, ch), F32), pltpu.VMEM((tm, ch), F32), pltpu.VMEM((8, ch), F32)],
        compiler_params=_params("parallel", "arbitrary"),
        name="rglru_bwd" if reverse else "rglru_fwd",
    )(p, p, p, cw, cb.reshape(1, ch), wa, br.reshape(1, ch), wx, bi.reshape(1, ch), lam.reshape(1, ch), h0)


def _hy_conv_kernel(prev_ref, cur_ref, next_ref, w_ref, b_ref, o_ref):
    i = pl.program_id(2)
    tm = cur_ref.shape[0]
    e = _with_halo(prev_ref, cur_ref, next_ref, i, pl.num_programs(2))
    w = w_ref[...]
    z = b_ref[...] + sum(_shift_rows(e, 1 - j) * w[j:j + 1] for j in range(3))
    o_ref[...] = z[HALO:HALO + tm]


def _hy_conv_call(p, w, bias):
    b, l, _ = p.shape
    ch = 256
    tm = min(l, 512)
    nt = l // tm
    specs = _halo_specs(tm, ch, 0, nt, nlead=3)
    def with_col(spec):
        inner = spec.index_map
        return pl.BlockSpec(spec.block_shape, lambda bi, j, i: inner(bi, j, i)[:2] + (5 + j,))
    return pl.pallas_call(
        _hy_conv_kernel,
        grid=(b, 3, nt),
        in_specs=[with_col(s) for s in specs] + [
            pl.BlockSpec((3, ch), lambda bi, j, i: (0, j)),
            pl.BlockSpec((1, ch), lambda bi, j, i: (0, j))],
        out_specs=pl.BlockSpec((None, None, tm, ch), lambda bi, j, i: (j, bi, i, 0)),
        out_shape=jax.ShapeDtypeStruct((3, b, l, ch), F32),
        compiler_params=_params("parallel", "parallel", "parallel"),
        name="hyena_short_conv",
    )(p, p, p, w, bias.reshape(1, -1))


def _hy_filter_kernel(emb_ref, w1_ref, b1_ref, fq_ref, w2_ref, b2_ref, w3_ref, dl_ref, f_ref, nrm_ref, *, seq_len):
    i = pl.program_id(0)
    tr = emb_ref.shape[0]
    emb = emb_ref[...]
    fq = fq_ref[...]
    h = jnp.sin(fq * (jnp.dot(emb, w1_ref[...], precision=HIGHEST, preferred_element_type=F32) + b1_ref[...]))
    h = jnp.sin(fq * (jnp.dot(h, w2_ref[...], precision=HIGHEST, preferred_element_type=F32) + b2_ref[...]))
    h = jnp.dot(h, w3_ref[...], precision=HIGHEST, preferred_element_type=F32)
    half = h.shape[1] // 2
    n = i * tr + lax.broadcasted_iota(jnp.int32, (tr, 1), 0)
    filt = jnp.where(n < seq_len, h[:, :half], jnp.where(n > seq_len, h[:, half:], 0.0))
    filt = filt * jnp.exp(-emb[:, 0:1] * dl_ref[...])
    f_ref[...] = filt

    @pl.when(i == 0)
    def _():
        nrm_ref[...] = jnp.zeros(nrm_ref.shape, F32)

    nrm_ref[...] += jnp.sum(jnp.abs(filt), axis=0, keepdims=True)


def _hy_filter_call(emb_circ, w1, b1, fq, w2, b2, w3, deltas2, seq_len):
    n2l = emb_circ.shape[0]
    hid = w2.shape[0]
    cols = w3.shape[1] // 2
    tr = min(n2l, 512)
    full = lambda shape: pl.BlockSpec(shape, lambda i: (0, 0))
    return pl.pallas_call(
        functools.partial(_hy_filter_kernel, seq_len=seq_len),
        grid=(n2l // tr,),
        in_specs=[pl.BlockSpec((tr, HY_EMB_PAD), lambda i: (i, 0)),
                  full((HY_EMB_PAD, hid)), full((1, hid)), full((1, hid)),
                  full((hid, hid)), full((1, hid)), full((hid, 2 * cols)), full((1, cols))],
        out_specs=[pl.BlockSpec((tr, cols), lambda i: (i, 0)), full((1, cols))],
        out_shape=[jax.ShapeDtypeStruct((n2l, cols), F32), jax.ShapeDtypeStruct((1, cols), F32)],
        compiler_params=_params("arbitrary"),
        name="hyena_filter",
    )(emb_circ, w1, b1.reshape(1, hid), fq.reshape(1, hid), w2, b2.reshape(1, hid), w3, deltas2)


def _dft_outer_kernel(f_ref, x_ref, o_ref):
    o_ref[...] = jnp.dot(f_ref[...], x_ref[...].astype(BF16), preferred_element_type=F32)


def _dft_outer_call(fmat, x):
    b, k, cols = x.shape
    m = fmat.shape[0]
    tn = min(cols, 2048)
    return pl.pallas_call(
        _dft_outer_kernel,
        grid=(b, cols // tn),
        in_specs=[pl.BlockSpec((m, k), lambda bi, j: (0, 0)),
                  pl.BlockSpec((None, k, tn), lambda bi, j: (bi, 0, j))],
        out_specs=pl.BlockSpec((None, m, tn), lambda bi, j: (bi, 0, j)),
        out_shape=jax.ShapeDtypeStruct((b, m, cols), F32),
        compiler_params=_params("parallel", "parallel"),
        name="dft_outer",
    )(fmat, x)


def _cmul(ar, ai, br, bi):
    return ar * br - ai * bi, ar * bi + ai * br


def _inner_dft(fblk, re, im):
    n2 = re.shape[0]
    z = jnp.dot(fblk, jnp.concatenate([re, im], axis=0).astype(BF16), preferred_element_type=F32)
    return z[:n2], z[n2:]


def _tile_lanes(t, width):
    return jnp.concatenate([t] * (width // t.shape[1]), axis=1)


def _hy_spec_kernel(a_ref, tw_ref, fblk_ref, nrm_ref, h_ref):
    cols = a_ref.shape[-1]
    ar, ai = _cmul(a_ref[0], a_ref[1], _tile_lanes(tw_ref[0], cols), _tile_lanes(tw_ref[1], cols))
    zr, zi = _inner_dft(fblk_ref[...], ar, ai)
    inv = 1.0 / nrm_ref[...]
    h_ref[0] = zr * inv
    h_ref[1] = zi * inv


def _hy_spec_call(a, twb, fblk, nrm):
    _, n1, n2, cols = a.shape
    return pl.pallas_call(
        _hy_spec_kernel,
        grid=(n1,),
        in_specs=[pl.BlockSpec((2, None, n2, cols), lambda k: (0, k, 0, 0)),
                  pl.BlockSpec((2, None, n2, 128), lambda k: (0, k, 0, 0)),
                  pl.BlockSpec((2 * n2, 2 * n2), lambda k: (0, 0)),
                  pl.BlockSpec((1, cols), lambda k: (0, 0))],
        out_specs=pl.BlockSpec((None, 2, n2, cols), lambda k: (k, 0, 0, 0)),
        out_shape=jax.ShapeDtypeStruct((n1, 2, n2, cols), F32),
        compiler_params=_params("parallel"),
        name="hyena_filter_spectrum",
    )(a, twb, fblk, nrm)


def _hy_mid_kernel(a_ref, tw_ref, fblk_ref, fblkt_ref, h_ref, o_ref):
    ch = a_ref.shape[-1]
    twr = _tile_lanes(tw_ref[0], ch)
    twi = _tile_lanes(tw_ref[1], ch)
    ar, ai = _cmul(a_ref[0], a_ref[1], twr, twi)
    zr, zi = _inner_dft(fblk_ref[...], ar, ai)
    yr, yi = _cmul(zr, zi, h_ref[0], h_ref[1])
    br, bi = _inner_dft(fblkt_ref[...], yr, yi)
    br, bi = _cmul(br, bi, twr, -twi)
    o_ref[0] = br
    o_ref[1] = bi


def _hy_mid_call(a, twb, fblk, fblkt, spec, order):
    b, _, n1, n2, ch = a.shape
    return pl.pallas_call(
        _hy_mid_kernel,
        grid=(b, n1),
        in_specs=[pl.BlockSpec((None, 2, None, n2, ch), lambda bi, k: (bi, 0, k, 0, 0)),
                  pl.BlockSpec((2, None, n2, 128), lambda bi, k: (0, k, 0, 0)),
                  pl.BlockSpec((2 * n2, 2 * n2), lambda bi, k: (0, 0)),
                  pl.BlockSpec((2 * n2, 2 * n2), lambda bi, k: (0, 0)),
                  pl.BlockSpec((None, 2, n2, ch), lambda bi, k: (k, 0, 0, order))],
        out_specs=pl.BlockSpec((None, 2, None, n2, ch), lambda bi, k: (bi, 0, k, 0, 0)),
        out_shape=jax.ShapeDtypeStruct(a.shape, F32),
        compiler_params=_params("parallel", "parallel"),
        name="hyena_spectral_mid",
    )(a, twb, fblk, fblkt, spec)


def _hy_out_kernel(f_ref, b_ref, u_ref, g_ref, sk_ref, o_ref):
    y = jnp.dot(f_ref[...], b_ref[...].astype(BF16), preferred_element_type=F32)
    o_ref[...] = g_ref[...] * (y + sk_ref[...] * u_ref[...])


def _hy_out_call(finv, bst, u, gate, skip_t):
    b, k, cols = bst.shape
    m = finv.shape[0]
    tn = min(cols, 2048)
    return pl.pallas_call(
        _hy_out_kernel,
        grid=(b, cols // tn),
        in_specs=[pl.BlockSpec((m, k), lambda bi, j: (0, 0)),
                  pl.BlockSpec((None, k, tn), lambda bi, j: (bi, 0, j)),
                  pl.BlockSpec((None, m, tn), lambda bi, j: (bi, 0, j)),
                  pl.BlockSpec((None, m, tn), lambda bi, j: (bi, 0, j)),
                  pl.BlockSpec((1, tn), lambda bi, j: (0, 0))],
        out_specs=pl.BlockSpec((None, m, tn), lambda bi, j: (bi, 0, j)),
        out_shape=jax.ShapeDtypeStruct((b, m, cols), F32),
        compiler_params=_params("parallel", "parallel"),
        name="hyena_dft_out",
    )(finv, bst, u, gate, skip_t)


def _hy_small_kernel(z_ref, filt_ref, nrm_ref, fs_ref, finv_ref, sk_ref, o_ref):
    l = z_ref.shape[1]
    ch = z_ref.shape[2]
    fs = fs_ref[...]
    n = fs.shape[0] // 2
    spec = jnp.dot(fs, filt_ref[...], precision=HIGHEST, preferred_element_type=F32) / nrm_ref[...]
    sig = z_ref[2]
    for order in range(2):
        u = jnp.dot(fs[:, :l], sig, precision=HIGHEST, preferred_element_type=F32)
        hr = spec[:n, order * ch:(order + 1) * ch]
        hi = spec[n:, order * ch:(order + 1) * ch]
        yr, yi = _cmul(u[:n], u[n:], hr, hi)
        y = jnp.dot(finv_ref[...], jnp.concatenate([yr, yi], axis=0), precision=HIGHEST, preferred_element_type=F32)
        sig = z_ref[order] * (y + sk_ref[order:order + 1] * sig)
    o_ref[...] = sig


def _hy_small_call(z, filt, nrm, fs, finv, skip):
    _, b, l, ch = z.shape
    full = lambda a: pl.BlockSpec(a.shape, lambda bi: (0,) * a.ndim)
    return pl.pallas_call(
        _hy_small_kernel,
        grid=(b,),
        in_specs=[pl.BlockSpec((3, None, l, ch), lambda bi: (0, bi, 0, 0)),
                  full(filt), full(nrm), full(fs), full(finv), full(skip)],
        out_specs=pl.BlockSpec((None, l, ch), lambda bi: (bi, 0, 0)),
        out_shape=jax.ShapeDtypeStruct((b, l, ch), F32),
        compiler_params=_params("parallel"),
        name="hyena_short_sequence",
    )(z, filt, nrm, fs, finv, skip)


def _post_kernel(x_ref, ya_ref, yb_ref, hf_ref, hb_ref, gt_ref, yd_ref, mod_ref, wo_ref, g1_ref, g2_ref, g3_ref,
                 w1_ref, w2_ref, o_ref):
    m = mod_ref[...]
    gt = gt_ref[...]
    gelu = 0.5 * gt * (1.0 + jnp.tanh(math.sqrt(2.0 / math.pi) * (gt + 0.044715 * gt * gt * gt)))
    yc = (hf_ref[...] + hb_ref[...]) * gelu
    cat = jnp.concatenate([ya_ref[...].astype(BF16), yb_ref[...], yc.astype(BF16), yd_ref[...].astype(BF16)], axis=-1)
    mix = jnp.dot(cat, wo_ref[...], preferred_element_type=F32)
    x1 = x_ref[...] + m[2:3] * _rms(mix, g1_ref[...])
    h = (_rms(x1, g2_ref[...]) * (1.0 + m[4:5]) + m[3:4]).astype(BF16)
    dff = w1_ref.shape[1]
    fc = 1024
    acc = jnp.zeros(x1.shape, F32)
    for c in range(dff // fc):
        a = jnp.maximum(jnp.dot(h, w1_ref[:, c * fc:(c + 1) * fc], preferred_element_type=F32), 0.0)
        acc = acc + jnp.dot((a * a).astype(BF16), w2_ref[c * fc:(c + 1) * fc, :], preferred_element_type=F32)
    o_ref[...] = x1 + m[5:6] * _rms(acc, g3_ref[...])


def _post_call(x, ya, yb, hf, hb, p, yd, mod, wo, g1, g2, g3, w1, w2):
    b, l, d = x.shape
    ch = ya.shape[-1]
    tm = min(l, 512)
    row = lambda w, col=0: pl.BlockSpec((None, tm, w), lambda bi, i: (bi, i, col))
    vec = pl.BlockSpec((1, d), lambda bi, i: (0, 0))
    return pl.pallas_call(
        _post_kernel,
        grid=(b, l // tm),
        in_specs=[row(d), row(ch), row(ch), row(ch), row(ch), row(ch, 4), row(ch),
                  pl.BlockSpec((None, 6, d), lambda bi, i: (bi, 0, 0)),
                  _resident(wo.shape, lambda bi, i: (0, 0)), vec, vec, vec,
                  _resident(w1.shape, lambda bi, i: (0, 0)),
                  _resident(w2.shape, lambda bi, i: (0, 0))],
        out_specs=row(d),
        out_shape=jax.ShapeDtypeStruct((b, l, d), F32),
        compiler_params=_params("parallel", "parallel"),
        name="out_proj_mlp",
    )(x, ya, yb, hf, hb, p, yd, mod, wo, g1.reshape(1, d), g2.reshape(1, d), g3.reshape(1, d), w1, w2)


def _rope_tables(seq_len):
    t = jnp.arange(seq_len)
    inv = ROPE_THETA ** (-jnp.arange(ROPE_FREQS, dtype=F32) / ROPE_FREQS)
    ang_r = (t // GRID_W).astype(F32)[:, None] * inv
    ang_c = (t % GRID_W).astype(F32)[:, None] * inv
    cos = jnp.concatenate([jnp.cos(ang_r)] * 2 + [jnp.cos(ang_c)] * 2, axis=1)
    sin = jnp.concatenate([-jnp.sin(ang_r), jnp.sin(ang_r), -jnp.sin(ang_c), jnp.sin(ang_c)], axis=1)
    return jnp.tile(cos, (1, 2)), jnp.tile(sin, (1, 2))


def _hyena_embedding(seq_len):
    t01 = jnp.linspace(0.0, 1.0, seq_len, dtype=F32)[:, None]
    w = 2.0 * math.pi * jnp.arange(seq_len, dtype=F32) / seq_len
    bands = jnp.linspace(1e-4, HY_BANDS - 1, HY_BANDS, dtype=F32)
    z = w[:, None] * bands[None, :]
    emb = jnp.concatenate([t01, jnp.cos(z), -jnp.sin(z)], axis=-1)
    circ = jnp.concatenate([emb, emb[:1], emb[:0:-1]], axis=0)
    return jnp.pad(circ, ((0, 0), (0, HY_EMB_PAD - HY_EMB)))


def _dft_cos_sin(rows, cols, n):
    ang = (2.0 * math.pi / n) * ((jnp.arange(rows)[:, None] * jnp.arange(cols)[None, :]) % n).astype(F32)
    return jnp.cos(ang), -jnp.sin(ang)


def _dft_tables(seq_len):
    n = 2 * seq_len
    n2 = DFT_N2
    n1 = n // n2
    fr, fi = _dft_cos_sin(n1, n1, n1)
    f_outer = jnp.concatenate([fr, fi], axis=0).astype(BF16)
    f_inv = (jnp.concatenate([fr, fi], axis=1)[:n1 // 2] * (1.0 / n)).astype(BF16)
    gr, gi = _dft_cos_sin(n2, n2, n2)
    fblk = jnp.concatenate([jnp.concatenate([gr, -gi], axis=1),
                            jnp.concatenate([gi, gr], axis=1)], axis=0).astype(BF16)
    tr, ti = _dft_cos_sin(n1, n2, n)
    twb = jnp.broadcast_to(jnp.stack([tr, ti])[..., None], (2, n1, n2, 128))
    return f_outer, f_inv, fblk, fblk.T, twb


def _dense_dft_tables(seq_len):
    n = 2 * seq_len
    fr, fi = _dft_cos_sin(n, n, n)
    fs = jnp.concatenate([fr, fi], axis=0)
    finv = jnp.concatenate([fr, fi], axis=1)[:seq_len] * (1.0 / n)
    return fs, finv


def _block_diag(w):
    g, c, _ = w.shape
    eye = jnp.eye(g, dtype=w.dtype)
    return (eye[:, None, :, None] * w[:, :, None, :]).reshape(g * c, g * c)


def _hyena_long(p, l_w, tables, filt, nrm):
    b, l, _ = p.shape
    f_outer, f_inv, fblk, fblkt, twb = tables
    n2 = DFT_N2
    n1 = 2 * l // n2
    ch = 256
    z = _hy_conv_call(p, l_w["hy_conv_w"], l_w["hy_conv_b"])
    fa = _dft_outer_call(f_outer, filt.reshape(1, n1, n2 * 2 * ch))
    spec = _hy_spec_call(fa.reshape(2, n1, n2, 2 * ch), twb, fblk, nrm)
    zf = z.reshape(3, b, n1 // 2, n2 * ch)
    skip_t = jnp.tile(l_w["hy_skip"], (1, min(n2, 2048 // ch)))
    sig = zf[2]
    for order in range(2):
        a = _dft_outer_call(f_outer[:, :n1 // 2], sig)
        bm = _hy_mid_call(a.reshape(b, 2, n1, n2, ch), twb, fblk, fblkt, spec, order)
        sig = _hy_out_call(f_inv, bm.reshape(b, 2 * n1, n2 * ch), sig, zf[order], skip_t[order:order + 1])
    return sig.reshape(b, l, ch)


def _layer_weights(args, l):
    return {k: v[l] for k, v in args.items()}


def kernel(x, c, ctx, c_ctx, w_mod, b_mod, g_pre_mix, g_post_mix, g_pre_mlp, g_post_mlp, w_in, w_out, pool_w, pool_scale, q_norm_g, k_norm_g, lru_conv_w, lru_conv_b, lru_wa, lru_ba, lru_wx, lru_bx, lru_lambda, hy_conv_w, hy_conv_b, hy_w1, hy_b1, hy_freq, hy_w2, hy_b2, hy_w3, hy_skip, mlp_w1, mlp_w2):
    b, l, d = x.shape
    lc = ctx.shape[1]
    depth = w_mod.shape[0]
    per_layer = dict(g_pre_mix=g_pre_mix, g_post_mix=g_post_mix, g_pre_mlp=g_pre_mlp, g_post_mlp=g_post_mlp,
                     w_in=w_in.astype(BF16), w_out=w_out.astype(BF16), pool_w=pool_w, pool_scale=pool_scale,
                     q_norm_g=q_norm_g, k_norm_g=k_norm_g, lru_conv_w=lru_conv_w, lru_conv_b=lru_conv_b,
                     lru_wa=lru_wa, lru_ba=lru_ba, lru_wx=lru_wx, lru_bx=lru_bx, lru_lambda=lru_lambda,
                     hy_conv_w=hy_conv_w, hy_conv_b=hy_conv_b, hy_w1=hy_w1, hy_b1=hy_b1, hy_freq=hy_freq,
                     hy_w2=hy_w2, hy_b2=hy_b2, hy_w3=hy_w3, hy_skip=hy_skip,
                     mlp_w1=mlp_w1.astype(BF16), mlp_w2=mlp_w2.astype(BF16))

    cos, sin = _rope_tables(l)
    avg = _block_diag(jnp.full((2, HEAD_DIM, HEAD_DIM), 1.0 / HEAD_DIM, F32)).astype(BF16)
    emb_x, emb_c = _hyena_embedding(l), _hyena_embedding(lc)
    dft_x = _dft_tables(l)
    fs_c, finv_c = _dense_dft_tables(lc)
    deltas = jnp.abs(jnp.linspace(HY_MIN_DECAY, HY_MAX_DECAY, 256, dtype=F32))
    deltas2 = jnp.tile(deltas, 2).reshape(1, 512)

    cond = jnp.zeros((8, d), F32).at[:b].set(c).at[b].set(c_ctx)
    mods = _mod_call(cond, w_mod, b_mod).reshape(depth, 8, 6, d)

    cx = ctx
    for li in range(depth):
        w = _layer_weights(per_layer, li)
        need_ctx = li < depth - 1
        mod_x = mods[li, :b]
        mod_c = jnp.broadcast_to(mods[li, b:b + 1], (b, 6, d))
        w1p = jnp.pad(w["hy_w1"], ((0, HY_EMB_PAD - HY_EMB), (0, 0)))
        pool_bd = _block_diag(w["pool_w"]).astype(BF16)
        gq = jnp.tile(w["q_norm_g"], 2).reshape(1, 128)
        gk = jnp.tile(w["k_norm_g"], 2).reshape(1, 128)
        wa = [_block_diag(w["lru_wa"][dr]).astype(BF16) for dr in range(2)]
        wx = [_block_diag(w["lru_wx"][dr]).astype(BF16) for dr in range(2)]

        px = _inproj_call(x, mod_x, w["g_pre_mix"], w["w_in"])
        pc = _inproj_call(cx, mod_c, w["g_pre_mix"], w["w_in"])

        qtx, kx, vtx = _qkv_call(px, cos, sin, gq, gk, avg)
        qtc, kc, vtc = _qkv_call(pc, None, None, gq, gk, avg)
        yb_x = _flash_call(qtx, kx, vtx, kc, vtc)

        hs_c, hs_x = [], []
        for dr in range(2):
            lw = (w["lru_conv_w"], w["lru_conv_b"], wa[dr], w["lru_ba"][dr], wx[dr], w["lru_bx"][dr],
                  w["lru_lambda"][dr])
            hc = _lru_call(pc, jnp.zeros((b, 1, 256), F32), *lw, reverse=bool(dr))
            h0 = hc[:, :1] if dr else hc[:, -1:]
            hs_c.append(hc)
            hs_x.append(_lru_call(px, h0, *lw, reverse=bool(dr)))

        ya_x = _pool_call(px, pool_bd, w["pool_scale"])
        hy = (w1p, w["hy_b1"], w["hy_freq"], w["hy_w2"], w["hy_b2"], w["hy_w3"], deltas2)
        filt_x, nrm_x = _hy_filter_call(emb_x, *hy, seq_len=l)
        yd_x = _hyena_long(px, w, dft_x, filt_x, nrm_x)

        x = _post_call(x, ya_x, yb_x, hs_x[0], hs_x[1], px, yd_x, mod_x, w["w_out"], w["g_post_mix"],
                       w["g_pre_mlp"], w["g_post_mlp"], w["mlp_w1"], w["mlp_w2"])

        if need_ctx:
            yb_c = _flash_call(qtc, kc, vtc)
            ya_c = _pool_call(pc, pool_bd, w["pool_scale"])
            filt_c, nrm_c = _hy_filter_call(emb_c, *hy, seq_len=lc)
            z_c = _hy_conv_call(pc, w["hy_conv_w"], w["hy_conv_b"])
            yd_c = _hy_small_call(z_c, filt_c, nrm_c, fs_c, finv_c, w["hy_skip"])
            cx = _post_call(cx, ya_c, yb_c, hs_c[0], hs_c[1], pc, yd_c, mod_c, w["w_out"], w["g_post_mix"],
                            w["g_pre_mlp"], w["g_post_mlp"], w["mlp_w1"], w["mlp_w2"])
    return x
```

```python
import functools
import math

import jax
import jax.numpy as jnp
from jax import lax
from jax.experimental import pallas as pl
from jax.experimental.pallas import tpu as pltpu

F32 = jnp.float32
BF16 = jnp.bfloat16
EPS = 1e-6

GRID_W = 64
HEAD_DIM = 64
QUERY_SCALE = HEAD_DIM ** -0.5 * math.log2(math.e)
N_KV_HEADS = 2
ROPE_FREQS = 16
ROPE_THETA = 10000.0
POOL_WINDOWS = (2, 4, 8, 16)
LRU_C = 8.0
HY_EMB = 33
HY_EMB_PAD = 64
HY_BANDS = 16
HY_MIN_DECAY = math.log(1e-2) / 1.5
HY_MAX_DECAY = math.log(1e-2) / 0.3
DFT_N2 = 128
HALO = 16
OUTER_FREQS_PER_STEP = 8
UNIT_KEYS = 1024
QUERY_CHAIN = 512
VMEM_LIMIT = 56 * 1024 * 1024
HIGHEST = lax.Precision.HIGHEST


def _params(*sem):
    return pltpu.CompilerParams(dimension_semantics=sem, vmem_limit_bytes=VMEM_LIMIT)


def _resident(shape, index_map):
    return pl.BlockSpec(shape, index_map, pipeline_mode=pl.Buffered(1))


def _rms(x, g):
    return x * lax.rsqrt(jnp.mean(x * x, axis=-1, keepdims=True) + EPS) * g


def _shift_rows(x, k):
    n = x.shape[0]
    return pltpu.roll(x, k % n, 0)


def _with_halo(prev_ref, cur_ref, next_ref, tile, ntiles):
    prev = jnp.where(tile > 0, prev_ref[...], 0.0)
    nxt = jnp.where(tile < ntiles - 1, next_ref[...], 0.0)
    return jnp.concatenate([prev, cur_ref[...], nxt], axis=0)


def _halo_specs(tm, width, col, ntiles, tile_of=lambda i: i, nlead=2):
    r = tm // HALO
    last = ntiles * r - 1

    def prev_map(*g):
        return (g[0], jnp.maximum(tile_of(g[nlead - 1]) * r - 1, 0), col)

    def cur_map(*g):
        return (g[0], tile_of(g[nlead - 1]), col)

    def next_map(*g):
        return (g[0], jnp.minimum((tile_of(g[nlead - 1]) + 1) * r, last), col)

    return [pl.BlockSpec((None, HALO, width), prev_map),
            pl.BlockSpec((None, tm, width), cur_map),
            pl.BlockSpec((None, HALO, width), next_map)]


def _mod_kernel(c_ref, w_ref, b_ref, o_ref):
    cond = c_ref[...]
    act = (cond * jax.nn.sigmoid(cond)).astype(BF16)
    o_ref[...] = jnp.dot(act, w_ref[...].astype(BF16), preferred_element_type=F32) + b_ref[...]


def _mod_call(cond, w_mod, b_mod):
    depth, d, n = w_mod.shape
    tn = 1024
    return pl.pallas_call(
        _mod_kernel,
        grid=(depth, n // tn),
        in_specs=[pl.BlockSpec((8, d), lambda l, j: (0, 0)),
                  pl.BlockSpec((None, d, tn), lambda l, j: (l, 0, j)),
                  pl.BlockSpec((None, 1, tn), lambda l, j: (l, 0, j))],
        out_specs=pl.BlockSpec((None, 8, tn), lambda l, j: (l, 0, j)),
        out_shape=jax.ShapeDtypeStruct((depth, 8, n), F32),
        compiler_params=_params("parallel", "parallel"),
        name="adaln_mod",
    )(cond, w_mod, b_mod.reshape(depth, 1, n))


def _inproj_kernel(x_ref, mod_ref, g_ref, w_ref, o_ref):
    m = mod_ref[...]
    h = _rms(x_ref[...], g_ref[...]) * (1.0 + m[1:2]) + m[0:1]
    o_ref[...] = jnp.dot(h.astype(BF16), w_ref[...], preferred_element_type=F32)


def _inproj_call(x, mod, g, w):
    b, l, d = x.shape
    n = w.shape[1]
    tm = min(l, 512)
    return pl.pallas_call(
        _inproj_kernel,
        grid=(b, l // tm),
        in_specs=[pl.BlockSpec((None, tm, d), lambda bi, i: (bi, i, 0)),
                  pl.BlockSpec((None, 6, d), lambda bi, i: (bi, 0, 0)),
                  pl.BlockSpec((1, d), lambda bi, i: (0, 0)),
                  _resident((d, n), lambda bi, i: (0, 0))],
        out_specs=pl.BlockSpec((None, tm, n), lambda bi, i: (bi, i, 0)),
        out_shape=jax.ShapeDtypeStruct((b, l, n), F32),
        compiler_params=_params("parallel", "parallel"),
        name="in_proj",
    )(x, mod, g.reshape(1, d), w)


def _pool_kernel(prev_ref, cur_ref, next_ref, w_ref, sc_ref, o_ref, *, seq_len):
    i = pl.program_id(1)
    tm, ch = cur_ref.shape
    e = _with_halo(prev_ref, cur_ref, next_ref, i, pl.num_programs(1))
    s2 = e + _shift_rows(e, 1)
    s4 = _shift_rows(s2, 1) + _shift_rows(s2, -1)
    s8 = _shift_rows(s4, 2) + _shift_rows(s4, -2)
    s16 = _shift_rows(s8, 4) + _shift_rows(s8, -4)
    t = i * tm + lax.broadcasted_iota(jnp.int32, (tm, 1), 0)
    group = lax.broadcasted_iota(jnp.int32, (1, ch), 1) // (ch // len(POOL_WINDOWS))
    mean = jnp.zeros((tm, ch), F32)
    for gi, (win, s) in enumerate(zip(POOL_WINDOWS, (s2, s4, s8, s16))):
        lo = jnp.maximum(t - win // 2, 0)
        hi = jnp.minimum(t + win - win // 2, seq_len)
        m = s[HALO:HALO + tm] / (hi - lo).astype(F32)
        mean = jnp.where(group == gi, m, mean)
    d = (mean - cur_ref[...]).astype(BF16)
    o_ref[...] = jnp.dot(d, w_ref[...], preferred_element_type=F32) * sc_ref[...]


def _pool_call(p, w_bd, scale):
    b, l, _ = p.shape
    ch = w_bd.shape[0]
    tm = min(l, 512)
    nt = l // tm
    return pl.pallas_call(
        functools.partial(_pool_kernel, seq_len=l),
        grid=(b, nt),
        in_specs=_halo_specs(tm, ch, 0, nt) + [
            pl.BlockSpec((ch, ch), lambda bi, i: (0, 0)),
            pl.BlockSpec((1, ch), lambda bi, i: (0, 0))],
        out_specs=pl.BlockSpec((None, tm, ch), lambda bi, i: (bi, i, 0)),
        out_shape=jax.ShapeDtypeStruct((b, l, ch), F32),
        compiler_params=_params("parallel", "parallel"),
        name="pool_mixer",
    )(p, p, p, w_bd, scale.reshape(1, ch))


def _head_norm(x, g, avg):
    sq = x * x
    hi = sq.astype(BF16)
    lo = (sq - hi.astype(F32)).astype(BF16)
    ms = (jnp.dot(hi, avg, preferred_element_type=F32) + jnp.dot(lo, avg, preferred_element_type=F32))
    return x * lax.rsqrt(ms + EPS) * g


def _rope(x, cos, sin):
    n = x.shape[1]
    lane = lax.broadcasted_iota(jnp.int32, (1, n), 1)
    partner = jnp.where(lane % 32 < 16, pltpu.roll(x, n - 16, 1), pltpu.roll(x, 16, 1))
    return x * cos + partner * sin


def _qkv_kernel(*refs, rope):
    if rope:
        q_ref, kv_ref, cos_ref, sin_ref, gq_ref, gk_ref, avg_ref, qt_ref, ko_ref, vt_ref = refs
    else:
        q_ref, kv_ref, gq_ref, gk_ref, avg_ref, qt_ref, ko_ref, vt_ref = refs
    avg = avg_ref[...]
    for half in range(2):
        q = _head_norm(q_ref[:, half * 128:(half + 1) * 128], gq_ref[...], avg)
        if rope:
            q = _rope(q, cos_ref[...], sin_ref[...])
        qt = (q * QUERY_SCALE).T
        for h in range(2):
            qt_ref[2 * half + h] = qt[h * HEAD_DIM:(h + 1) * HEAD_DIM].astype(BF16)
    k = _head_norm(kv_ref[:, 0:128], gk_ref[...], avg)
    if rope:
        k = _rope(k, cos_ref[...], sin_ref[...])
    ko_ref[...] = k.astype(BF16)
    vt = kv_ref[:, 128:256].T
    for h in range(N_KV_HEADS):
        vt_ref[h] = vt[h * HEAD_DIM:(h + 1) * HEAD_DIM].astype(BF16)


def _qkv_call(p, cos, sin, gq, gk, avg):
    b, l, _ = p.shape
    tm = min(l, 512)
    rope = cos is not None
    tables = [cos, sin] if rope else []
    table_specs = [pl.BlockSpec((tm, 128), lambda bi, i: (i, 0))] * 2 if rope else []
    const = pl.BlockSpec((1, 128), lambda bi, i: (0, 0))
    return pl.pallas_call(
        functools.partial(_qkv_kernel, rope=rope),
        grid=(b, l // tm),
        in_specs=[pl.BlockSpec((None, tm, 256), lambda bi, i: (bi, i, 1)),
                  pl.BlockSpec((None, tm, 256), lambda bi, i: (bi, i, 2))] + table_specs + [
                  const, const, pl.BlockSpec((128, 128), lambda bi, i: (0, 0))],
        out_specs=[pl.BlockSpec((None, 2 * N_KV_HEADS, HEAD_DIM, tm), lambda bi, i: (bi, 0, 0, i)),
                   pl.BlockSpec((None, tm, 128), lambda bi, i: (bi, i, 0)),
                   pl.BlockSpec((None, N_KV_HEADS, None, HEAD_DIM, tm), lambda bi, i: (bi, 0, i, 0, 0))],
        out_shape=[jax.ShapeDtypeStruct((b, 2 * N_KV_HEADS, HEAD_DIM, l), BF16),
                   jax.ShapeDtypeStruct((b, l, 128), BF16),
                   jax.ShapeDtypeStruct((b, N_KV_HEADS, l // tm, HEAD_DIM, tm), BF16)],
        compiler_params=_params("parallel", "parallel"),
        name="qkv_prep",
    )(p, p, *tables, gq, gk, avg)


def _flash_kernel(*refs, with_ctx):
    if with_ctx:
        q_ref, k_ref, vt_ref, kc_ref, vtc_ref, o_ref, qs, m_s, l_s, acc, sb0, sb1 = refs
    else:
        q_ref, k_ref, vt_ref, o_ref, qs, m_s, l_s, acc, sb0, sb1 = refs
    h = pl.program_id(1)
    kj = pl.program_id(3)
    tq = q_ref.shape[-1]
    qc = qs.shape[-1]
    per_head = tq // qc
    nchain = 2 * per_head
    nkc, _, tkc = vt_ref.shape
    cpu = min(nkc, UNIT_KEYS // tkc)
    ukeys = cpu * tkc
    nunits = nchain * (nkc // cpu)

    def sublane_all(op, x):
        for sh in (1, 2, 4):
            x = op(x, pltpu.roll(x, sh, 0))
        return x

    def softmax_pv(s, vts, c):
        s3 = s.reshape(s.shape[0] // 8, 8, qc)
        m_prev = m_s[c]
        m_new = jnp.maximum(m_prev, sublane_all(jnp.maximum, jnp.max(s3, axis=0)))
        alpha = jnp.exp2(m_prev - m_new)
        p3 = jnp.exp2(s3 - m_new[None])
        l_s[c] = alpha * l_s[c] + jnp.sum(p3, axis=0)
        p = p3.reshape(s.shape).astype(BF16)
        step = s.shape[0] // len(vts)
        pv = sum(jnp.dot(vt, p[i * step:(i + 1) * step], preferred_element_type=F32) for i, vt in enumerate(vts))
        acc[c] = (acc[c].reshape(HEAD_DIM // 8, 8, qc) * alpha[None]).reshape(HEAD_DIM, qc) + pv
        m_s[c] = m_new

    def scores(u, buf):
        rows = pl.ds(pl.multiple_of((u // nchain) * ukeys, ukeys), ukeys)
        buf[...] = jnp.dot(k_ref[rows, :], qs[u % nchain], preferred_element_type=F32)

    def consume(u, buf):
        softmax_pv(buf[...], [vt_ref[(u // nchain) * cpu + i] for i in range(cpu)], u % nchain)

    @pl.when(kj == 0)
    def _():
        qs[...] = jnp.zeros(qs.shape, BF16)
        rows = pl.ds(pl.multiple_of(h * HEAD_DIM, HEAD_DIM), HEAD_DIM)
        for c in range(nchain):
            qs[c, rows, :] = q_ref[c // per_head, :, (c % per_head) * qc:(c % per_head + 1) * qc]
        m_s[...] = jnp.full(m_s.shape, -1e30, F32)
        l_s[...] = jnp.zeros(l_s.shape, F32)
        acc[...] = jnp.zeros(acc.shape, F32)
        if with_ctx:
            for c in range(nchain):
                softmax_pv(jnp.dot(kc_ref[...], qs[c], preferred_element_type=F32), [vtc_ref[...]], c)

    scores(0, sb0)

    def pair(j, carry):
        scores(2 * j + 1, sb1)
        consume(2 * j, sb0)
        scores(2 * j + 2, sb0)
        consume(2 * j + 1, sb1)
        return carry

    lax.fori_loop(0, nunits // 2 - 1, pair, 0)
    scores(nunits - 1, sb1)
    consume(nunits - 2, sb0)
    consume(nunits - 1, sb1)

    @pl.when(kj == pl.num_programs(3) - 1)
    def _():
        for r in range(per_head):
            halves = []
            for c in (r, r + per_head):
                l = sublane_all(jnp.add, l_s[c])
                halves.append((acc[c].reshape(HEAD_DIM // 8, 8, qc) / l[None]).reshape(HEAD_DIM, qc))
            o_ref[r * qc:(r + 1) * qc, :] = jnp.concatenate(halves, axis=0).T.astype(o_ref.dtype)


def _flash_call(qt, k, vt, kc=None, vtc=None):
    b, _, _, lq = qt.shape
    lk = k.shape[1]
    tkc = vt.shape[-1]
    tq = min(lq, 512)
    tk = lk
    nkc = tk // tkc
    qc = min(tq, QUERY_CHAIN)
    nchain = 2 * tq // qc
    ukeys = min(nkc, UNIT_KEYS // tkc) * tkc
    with_ctx = kc is not None
    in_specs = [pl.BlockSpec((None, 2, HEAD_DIM, tq), lambda bi, h, i, j: (bi, h, 0, i)),
                pl.BlockSpec((None, tk, 128), lambda bi, h, i, j: (bi, j, 0)),
                pl.BlockSpec((None, None, nkc, HEAD_DIM, tkc), lambda bi, h, i, j: (bi, h, j, 0, 0))]
    args = [qt, k, vt]
    if with_ctx:
        c = kc.shape[1]
        in_specs += [pl.BlockSpec((None, c, 128), lambda bi, h, i, j: (bi, 0, 0)),
                     pl.BlockSpec((None, None, None, HEAD_DIM, c), lambda bi, h, i, j: (bi, h, 0, 0, 0))]
        args += [kc, vtc]
    return pl.pallas_call(
        functools.partial(_flash_kernel, with_ctx=with_ctx),
        grid=(b, N_KV_HEADS, lq // tq, lk // tk),
        in_specs=in_specs,
        out_specs=pl.BlockSpec((None, tq, 128), lambda bi, h, i, j: (bi, i, h)),
        out_shape=jax.ShapeDtypeStruct((b, lq, 256), BF16),
        scratch_shapes=[pltpu.VMEM((nchain, 2 * HEAD_DIM, qc), BF16),
                        pltpu.VMEM((nchain, 8, qc), F32),
                        pltpu.VMEM((nchain, 8, qc), F32),
                        pltpu.VMEM((nchain, HEAD_DIM, qc), F32),
                        pltpu.VMEM((ukeys, qc), F32),
                        pltpu.VMEM((ukeys, qc), F32)],
        compiler_params=_params("parallel", "parallel", "parallel", "arbitrary"),
        name="flash_attention",
    )(*args)


def _lru_kernel(prev_ref, cur_ref, next_ref, cw_ref, cb_ref, wa_ref, br_ref, wx_ref, bi_ref, lam_ref, h0_ref,
                o_ref, a_s, b_s, carry, *, reverse):
    i = pl.program_id(1)
    nt = pl.num_programs(1)
    tm, ch = cur_ref.shape
    tile = nt - 1 - i if reverse else i

    @pl.when(i == 0)
    def _():
        carry[...] = jnp.broadcast_to(h0_ref[...], carry.shape)

    e = _with_halo(prev_ref, cur_ref, next_ref, tile, nt)
    cw = cw_ref[...]
    xc = cb_ref[...] + sum(_shift_rows(e, 2 - j) * cw[j:j + 1] for j in range(4))
    xc = xc[HALO:HALO + tm]
    xb = xc.astype(BF16)
    r = jax.nn.sigmoid(jnp.dot(xb, wa_ref[...], preferred_element_type=F32) + br_ref[...])
    gate_i = jax.nn.sigmoid(jnp.dot(xb, wx_ref[...], preferred_element_type=F32) + bi_ref[...])
    nl = -lam_ref[...]
    softplus = jnp.maximum(nl, 0.0) + jnp.log1p(jnp.exp(-jnp.abs(nl)))
    a = jnp.exp(-LRU_C * r * softplus)
    bb = jnp.sqrt(1.0 - a * a) * (gate_i * xc)

    row = lax.broadcasted_iota(jnp.int32, (tm, 1), 0) % 8
    for k in (1, 2, 4):
        sh = -k if reverse else k
        take = (row < 8 - k) if reverse else (row >= k)
        bb = jnp.where(take, a * _shift_rows(bb, sh) + bb, bb)
        a = jnp.where(take, a * _shift_rows(a, sh), a)
    a_s[...] = a
    b_s[...] = bb
    ngroups = tm // 8

    def body(g, hprev):
        gg = ngroups - 1 - g if reverse else g
        sl = pl.ds(pl.multiple_of(gg * 8, 8), 8)
        rows = b_s[sl, :] + a_s[sl, :] * hprev
        o_ref[sl, :] = rows
        edge = rows[0:1] if reverse else rows[7:8]
        return jnp.broadcast_to(edge, (8, ch))

    carry[...] = lax.fori_loop(0, ngroups, body, carry[...])


def _lru_call(p, h0, cw, cb, wa, br, wx, bi, lam, reverse):
    b, l, _ = p.shape
    ch = wa.shape[0]
    tm = min(l, 512)
    nt = l // tm
    tile_of = (lambda i: nt - 1 - i) if reverse else (lambda i: i)
    vec = pl.BlockSpec((1, ch), lambda bi_, i: (0, 0))
    mat = pl.BlockSpec((ch, ch), lambda bi_, i: (0, 0))
    return pl.pallas_call(
        functools.partial(_lru_kernel, reverse=reverse),
        grid=(b, nt),
        in_specs=_halo_specs(tm, ch, 3, nt, tile_of) + [
            pl.BlockSpec((4, ch), lambda bi_, i: (0, 0)), vec, mat, vec, mat, vec, vec,
            pl.BlockSpec((None, 1, ch), lambda bi_, i: (bi_, 0, 0))],
        out_specs=pl.BlockSpec((None, tm, ch), lambda bi_, i: (bi_, tile_of(i), 0)),
        out_shape=jax.ShapeDtypeStruct((b, l, ch), F32),
        scratch_shapes=[pltpu.VMEM((tm, ch), F32), pltpu.VMEM((tm, ch), F32), pltpu.VMEM((8, ch), F32)],
        compiler_params=_params("parallel", "arbitrary"),
        name="rglru_bwd" if reverse else "rglru_fwd",
    )(p, p, p, cw, cb.reshape(1, ch), wa, br.reshape(1, ch), wx, bi.reshape(1, ch), lam.reshape(1, ch), h0)


def _hy_conv_kernel(prev_ref, cur_ref, next_ref, w_ref, b_ref, o_ref):
    i = pl.program_id(2)
    tm = cur_ref.shape[0]
    e = _with_halo(prev_ref, cur_ref, next_ref, i, pl.num_programs(2))
    w = w_ref[...]
    z = b_ref[...] + sum(_shift_rows(e, 1 - j) * w[j:j + 1] for j in range(3))
    o_ref[...] = z[HALO:HALO + tm]


def _hy_conv_call(p, w, bias):
    b, l, _ = p.shape
    ch = 256
    tm = min(l, 512)
    nt = l // tm
    specs = _halo_specs(tm, ch, 0, nt, nlead=3)
    def with_col(spec):
        inner = spec.index_map
        return pl.BlockSpec(spec.block_shape, lambda bi, j, i: inner(bi, j, i)[:2] + (5 + j,))
    return pl.pallas_call(
        _hy_conv_kernel,
        grid=(b, 3, nt),
        in_specs=[with_col(s) for s in specs] + [
            pl.BlockSpec((3, ch), lambda bi, j, i: (0, j)),
            pl.BlockSpec((1, ch), lambda bi, j, i: (0, j))],
        out_specs=pl.BlockSpec((None, None, tm, ch), lambda bi, j, i: (j, bi, i, 0)),
        out_shape=jax.ShapeDtypeStruct((3, b, l, ch), F32),
        compiler_params=_params("parallel", "parallel", "parallel"),
        name="hyena_short_conv",
    )(p, p, p, w, bias.reshape(1, -1))


def _dot_bf16x3(a, b):
    a_hi = a.astype(BF16)
    b_hi = b.astype(BF16)
    a_lo = (a - a_hi.astype(F32)).astype(BF16)
    b_lo = (b - b_hi.astype(F32)).astype(BF16)
    dot = functools.partial(jnp.dot, preferred_element_type=F32)
    return dot(a_hi, b_hi) + dot(a_lo, b_hi) + dot(a_hi, b_lo)


def _hy_filter_kernel(emb_ref, w1_ref, b1_ref, fq_ref, w2_ref, b2_ref, w3_ref, dl_ref, f_ref, nrm_ref, *, seq_len):
    i = pl.program_id(0)
    tr = emb_ref.shape[0]
    emb = emb_ref[...]
    fq = fq_ref[...]
    h = jnp.sin(fq * (jnp.dot(emb, w1_ref[...], precision=HIGHEST, preferred_element_type=F32) + b1_ref[...]))
    h = jnp.sin(fq * (jnp.dot(h, w2_ref[...], precision=HIGHEST, preferred_element_type=F32) + b2_ref[...]))
    h = _dot_bf16x3(h, w3_ref[...])
    half = h.shape[1] // 2
    n = i * tr + lax.broadcasted_iota(jnp.int32, (tr, 1), 0)
    filt = jnp.where(n < seq_len, h[:, :half], jnp.where(n > seq_len, h[:, half:], 0.0))
    filt = filt * jnp.exp(-emb[:, 0:1] * dl_ref[...])
    f_ref[...] = filt

    @pl.when(i == 0)
    def _():
        nrm_ref[...] = jnp.zeros(nrm_ref.shape, F32)

    nrm_ref[...] += jnp.sum(jnp.abs(filt), axis=0, keepdims=True)


def _hy_filter_call(emb_circ, w1, b1, fq, w2, b2, w3, deltas2, seq_len):
    n2l = emb_circ.shape[0]
    hid = w2.shape[0]
    cols = w3.shape[1] // 2
    tr = min(n2l, 512)
    full = lambda shape: pl.BlockSpec(shape, lambda i: (0, 0))
    return pl.pallas_call(
        functools.partial(_hy_filter_kernel, seq_len=seq_len),
        grid=(n2l // tr,),
        in_specs=[pl.BlockSpec((tr, HY_EMB_PAD), lambda i: (i, 0)),
                  full((HY_EMB_PAD, hid)), full((1, hid)), full((1, hid)),
                  full((hid, hid)), full((1, hid)), full((hid, 2 * cols)), full((1, cols))],
        out_specs=[pl.BlockSpec((tr, cols), lambda i: (i, 0)), full((1, cols))],
        out_shape=[jax.ShapeDtypeStruct((n2l, cols), F32), jax.ShapeDtypeStruct((1, cols), F32)],
        compiler_params=_params("arbitrary"),
        name="hyena_filter",
    )(emb_circ, w1, b1.reshape(1, hid), fq.reshape(1, hid), w2, b2.reshape(1, hid), w3, deltas2)


def _outer_matmuls(fmat, x):
    xt = jnp.swapaxes(x, 0, 1).astype(BF16)
    ys = [jnp.dot(fmat, xt[s], preferred_element_type=F32) for s in range(x.shape[1])]
    return jnp.swapaxes(jnp.stack(ys, axis=0), 0, 1)


def _dft_outer_kernel(f_ref, x_ref, o_ref):
    m = o_ref.shape[1]
    y = _outer_matmuls(f_ref[...], x_ref[...])
    o_ref[0] = y[:m]
    o_ref[1] = y[m:]


def _dft_outer_call(fmat, x, lead):
    _, g, k, n2, w = x.shape
    m = fmat.shape[0] // 2
    return pl.pallas_call(
        _dft_outer_kernel,
        grid=(g, n2 // 8, w // 256),
        in_specs=[pl.BlockSpec((2 * m, k), lambda gi, j, c: (0, 0)),
                  pl.BlockSpec((None, None, k, 8, 256), lambda gi, j, c: (lead, gi, 0, j, c))],
        out_specs=pl.BlockSpec((None, 2, m, 8, 256), lambda gi, j, c: (gi, 0, 0, j, c)),
        out_shape=jax.ShapeDtypeStruct((g, 2, m, n2, w), F32),
        compiler_params=_params("parallel", "parallel", "parallel"),
        name="dft_outer",
    )(fmat, x)


def _cmul(ar, ai, br, bi):
    return ar * br - ai * bi, ar * bi + ai * br


def _inner_dft(fblk, re, im):
    n2 = re.shape[0]
    z = jnp.dot(fblk, jnp.concatenate([re, im], axis=0).astype(BF16), preferred_element_type=F32)
    return z[:n2], z[n2:]


def _tile_lanes(t, width):
    return jnp.concatenate([t] * (width // t.shape[1]), axis=1)


def _hy_spec_kernel(a_ref, tw_ref, fblk_ref, nrm_ref, h_ref):
    cols = a_ref.shape[-1]
    inv = 1.0 / nrm_ref[...]
    for k in range(a_ref.shape[1]):
        ar, ai = _cmul(a_ref[0, k], a_ref[1, k], _tile_lanes(tw_ref[0, k], cols), _tile_lanes(tw_ref[1, k], cols))
        zr, zi = _inner_dft(fblk_ref[...], ar, ai)
        h_ref[k, 0] = zr * inv
        h_ref[k, 1] = zi * inv


def _hy_spec_call(a, twb, fblk, nrm):
    _, _, n1, n2, cols = a.shape
    kb = min(n1, OUTER_FREQS_PER_STEP)
    return pl.pallas_call(
        _hy_spec_kernel,
        grid=(n1 // kb,),
        in_specs=[pl.BlockSpec((None, 2, kb, n2, cols), lambda k: (0, 0, k, 0, 0)),
                  pl.BlockSpec((2, kb, n2, 128), lambda k: (0, k, 0, 0)),
                  pl.BlockSpec((2 * n2, 2 * n2), lambda k: (0, 0)),
                  pl.BlockSpec((1, cols), lambda k: (0, 0))],
        out_specs=pl.BlockSpec((kb, 2, n2, cols), lambda k: (k, 0, 0, 0)),
        out_shape=jax.ShapeDtypeStruct((n1, 2, n2, cols), F32),
        compiler_params=_params("parallel"),
        name="hyena_filter_spectrum",
    )(a, twb, fblk, nrm)


def _hy_mid_kernel(a_ref, tw_ref, fblk_ref, fblkt_ref, h_ref, o_ref):
    ch = a_ref.shape[-1]
    for k in range(a_ref.shape[1]):
        twr = _tile_lanes(tw_ref[0, k], ch)
        twi = _tile_lanes(tw_ref[1, k], ch)
        ar, ai = _cmul(a_ref[0, k], a_ref[1, k], twr, twi)
        zr, zi = _inner_dft(fblk_ref[...], ar, ai)
        yr, yi = _cmul(zr, zi, h_ref[k, 0], h_ref[k, 1])
        br, bi = _inner_dft(fblkt_ref[...], yr, yi)
        br, bi = _cmul(br, bi, twr, -twi)
        o_ref[0, k] = br
        o_ref[1, k] = bi


def _hy_mid_call(a, twb, fblk, fblkt, spec, order):
    b, _, n1, n2, ch = a.shape
    kb = min(n1, OUTER_FREQS_PER_STEP)
    return pl.pallas_call(
        _hy_mid_kernel,
        grid=(b, n1 // kb),
        in_specs=[pl.BlockSpec((None, 2, kb, n2, ch), lambda bi, k: (bi, 0, k, 0, 0)),
                  pl.BlockSpec((2, kb, n2, 128), lambda bi, k: (0, k, 0, 0)),
                  pl.BlockSpec((2 * n2, 2 * n2), lambda bi, k: (0, 0)),
                  pl.BlockSpec((2 * n2, 2 * n2), lambda bi, k: (0, 0)),
                  pl.BlockSpec((kb, 2, n2, ch), lambda bi, k: (k, 0, 0, order))],
        out_specs=pl.BlockSpec((None, 2, kb, n2, ch), lambda bi, k: (bi, 0, k, 0, 0)),
        out_shape=jax.ShapeDtypeStruct(a.shape, F32),
        compiler_params=_params("parallel", "parallel"),
        name="hyena_spectral_mid",
    )(a, twb, fblk, fblkt, spec)


def _hy_out_kernel(f_ref, b_ref, u_ref, g_ref, sk_ref, o_ref):
    _, n1, s, ch = b_ref.shape
    y = _outer_matmuls(f_ref[...], b_ref[...].reshape(2 * n1, s, ch))
    o_ref[...] = g_ref[...] * (y + sk_ref[...] * u_ref[...])


def _hy_out_call(finv, bm, u, lead, z, order, skip):
    b, _, n1, n2, ch = bm.shape
    m = finv.shape[0]
    return pl.pallas_call(
        _hy_out_kernel,
        grid=(b, n2 // 8),
        in_specs=[pl.BlockSpec((m, 2 * n1), lambda bi, j: (0, 0)),
                  pl.BlockSpec((None, 2, n1, 8, ch), lambda bi, j: (bi, 0, 0, j, 0)),
                  pl.BlockSpec((None, None, m, 8, ch), lambda bi, j: (lead, bi, 0, j, 0)),
                  pl.BlockSpec((None, None, m, 8, ch), lambda bi, j: (order, bi, 0, j, 0)),
                  pl.BlockSpec((None, 1, ch), lambda bi, j: (order, 0, 0))],
        out_specs=pl.BlockSpec((None, m, 8, ch), lambda bi, j: (bi, 0, j, 0)),
        out_shape=jax.ShapeDtypeStruct((b, m, n2, ch), F32),
        compiler_params=_params("parallel", "parallel"),
        name="hyena_dft_out",
    )(finv, bm, u, z, skip)


def _hy_small_kernel(z_ref, filt_ref, nrm_ref, fs_ref, finv_ref, sk_ref, o_ref):
    l = z_ref.shape[1]
    ch = z_ref.shape[2]
    fs = fs_ref[...]
    n = fs.shape[0] // 2
    spec = jnp.dot(fs, filt_ref[...], precision=HIGHEST, preferred_element_type=F32) / nrm_ref[...]
    sig = z_ref[2]
    for order in range(2):
        u = jnp.dot(fs[:, :l], sig, precision=HIGHEST, preferred_element_type=F32)
        hr = spec[:n, order * ch:(order + 1) * ch]
        hi = spec[n:, order * ch:(order + 1) * ch]
        yr, yi = _cmul(u[:n], u[n:], hr, hi)
        y = jnp.dot(finv_ref[...], jnp.concatenate([yr, yi], axis=0), precision=HIGHEST, preferred_element_type=F32)
        sig = z_ref[order] * (y + sk_ref[order:order + 1] * sig)
    o_ref[...] = sig


def _hy_small_call(z, filt, nrm, fs, finv, skip):
    _, b, l, ch = z.shape
    full = lambda a: pl.BlockSpec(a.shape, lambda bi: (0,) * a.ndim)
    return pl.pallas_call(
        _hy_small_kernel,
        grid=(b,),
        in_specs=[pl.BlockSpec((3, None, l, ch), lambda bi: (0, bi, 0, 0)),
                  full(filt), full(nrm), full(fs), full(finv), full(skip)],
        out_specs=pl.BlockSpec((None, l, ch), lambda bi: (bi, 0, 0)),
        out_shape=jax.ShapeDtypeStruct((b, l, ch), F32),
        compiler_params=_params("parallel"),
        name="hyena_short_sequence",
    )(z, filt, nrm, fs, finv, skip)


def _post_kernel(x_ref, ya_ref, yb_ref, hf_ref, hb_ref, gt_ref, yd_ref, mod_ref, wo_ref, g1_ref, g2_ref, g3_ref,
                 w1_ref, w2_ref, o_ref):
    m = mod_ref[...]
    gt = gt_ref[...]
    gelu = 0.5 * gt * (1.0 + jnp.tanh(math.sqrt(2.0 / math.pi) * (gt + 0.044715 * gt * gt * gt)))
    yc = (hf_ref[...] + hb_ref[...]) * gelu
    cat = jnp.concatenate([ya_ref[...].astype(BF16), yb_ref[...], yc.astype(BF16), yd_ref[...].astype(BF16)], axis=-1)
    mix = jnp.dot(cat, wo_ref[...], preferred_element_type=F32)
    x1 = x_ref[...] + m[2:3] * _rms(mix, g1_ref[...])
    h = (_rms(x1, g2_ref[...]) * (1.0 + m[4:5]) + m[3:4]).astype(BF16)
    dff = w1_ref.shape[1]
    fc = 1024
    acc = jnp.zeros(x1.shape, F32)
    for c in range(dff // fc):
        a = jnp.maximum(jnp.dot(h, w1_ref[:, c * fc:(c + 1) * fc], preferred_element_type=F32), 0.0)
        acc = acc + jnp.dot((a * a).astype(BF16), w2_ref[c * fc:(c + 1) * fc, :], preferred_element_type=F32)
    o_ref[...] = x1 + m[5:6] * _rms(acc, g3_ref[...])


def _post_call(x, ya, yb, hf, hb, p, yd, mod, wo, g1, g2, g3, w1, w2):
    b, l, d = x.shape
    ch = ya.shape[-1]
    tm = min(l, 512)
    row = lambda w, col=0: pl.BlockSpec((None, tm, w), lambda bi, i: (bi, i, col))
    vec = pl.BlockSpec((1, d), lambda bi, i: (0, 0))
    return pl.pallas_call(
        _post_kernel,
        grid=(b, l // tm),
        in_specs=[row(d), row(ch), row(ch), row(ch), row(ch), row(ch, 4), row(ch),
                  pl.BlockSpec((None, 6, d), lambda bi, i: (bi, 0, 0)),
                  _resident(wo.shape, lambda bi, i: (0, 0)), vec, vec, vec,
                  _resident(w1.shape, lambda bi, i: (0, 0)),
                  _resident(w2.shape, lambda bi, i: (0, 0))],
        out_specs=row(d),
        out_shape=jax.ShapeDtypeStruct((b, l, d), F32),
        compiler_params=_params("parallel", "parallel"),
        name="out_proj_mlp",
    )(x, ya, yb, hf, hb, p, yd, mod, wo, g1.reshape(1, d), g2.reshape(1, d), g3.reshape(1, d), w1, w2)


def _rope_tables(seq_len):
    t = jnp.arange(seq_len)
    inv = ROPE_THETA ** (-jnp.arange(ROPE_FREQS, dtype=F32) / ROPE_FREQS)
    ang_r = (t // GRID_W).astype(F32)[:, None] * inv
    ang_c = (t % GRID_W).astype(F32)[:, None] * inv
    cos = jnp.concatenate([jnp.cos(ang_r)] * 2 + [jnp.cos(ang_c)] * 2, axis=1)
    sin = jnp.concatenate([-jnp.sin(ang_r), jnp.sin(ang_r), -jnp.sin(ang_c), jnp.sin(ang_c)], axis=1)
    return jnp.tile(cos, (1, 2)), jnp.tile(sin, (1, 2))


def _hyena_embedding(seq_len):
    def circular(v):
        return jnp.concatenate([v, v[:1], v[:0:-1]])

    t01 = circular(jnp.linspace(0.0, 1.0, seq_len, dtype=F32))[:, None]
    w = circular(2.0 * math.pi * jnp.arange(seq_len, dtype=F32) / seq_len)
    bands = jnp.linspace(1e-4, HY_BANDS - 1, HY_BANDS, dtype=F32)
    z = w[:, None] * bands[None, :]
    pad = jnp.zeros((2 * seq_len, HY_EMB_PAD - HY_EMB), F32)
    return jnp.concatenate([t01, jnp.cos(z), -jnp.sin(z), pad], axis=-1)


def _dft_cos_sin(rows, cols, n):
    ang = (2.0 * math.pi / n) * ((jnp.arange(rows)[:, None] * jnp.arange(cols)[None, :]) % n).astype(F32)
    return jnp.cos(ang), -jnp.sin(ang)


def _dft_tables(seq_len):
    n = 2 * seq_len
    n2 = DFT_N2
    n1 = n // n2
    fr, fi = _dft_cos_sin(n1, n1, n1)
    f_outer = jnp.concatenate([fr, fi], axis=0).astype(BF16)
    f_inv = (jnp.concatenate([fr, fi], axis=1)[:n1 // 2] * (1.0 / n)).astype(BF16)
    gr, gi = _dft_cos_sin(n2, n2, n2)
    fblk = jnp.concatenate([jnp.concatenate([gr, -gi], axis=1),
                            jnp.concatenate([gi, gr], axis=1)], axis=0).astype(BF16)
    tr, ti = _dft_cos_sin(n1, n2, n)
    twb = jnp.broadcast_to(jnp.stack([tr, ti])[..., None], (2, n1, n2, 128))
    return f_outer, f_inv, fblk, fblk.T, twb


def _dense_dft_tables(seq_len):
    n = 2 * seq_len
    fr, fi = _dft_cos_sin(n, n, n)
    fs = jnp.concatenate([fr, fi], axis=0)
    finv = jnp.concatenate([fr, fi], axis=1)[:seq_len] * (1.0 / n)
    return fs, finv


def _block_diag(w):
    g, c, _ = w.shape
    eye = jnp.eye(g, dtype=w.dtype)
    return (eye[:, None, :, None] * w[:, :, None, :]).reshape(g * c, g * c)


def _hyena_long(p, l_w, tables, filt, nrm):
    b, l, _ = p.shape
    f_outer, f_inv, fblk, fblkt, twb = tables
    n2 = DFT_N2
    n1 = 2 * l // n2
    ch = 256
    z = _hy_conv_call(p, l_w["hy_conv_w"], l_w["hy_conv_b"]).reshape(3, b, n1 // 2, n2, ch)
    fa = _dft_outer_call(f_outer, filt.reshape(1, 1, n1, n2, 2 * ch), 0)
    spec = _hy_spec_call(fa, twb, fblk, nrm)
    skip = l_w["hy_skip"].reshape(2, 1, ch)
    sig, lead = z, 2
    for order in range(2):
        a = _dft_outer_call(f_outer[:, :n1 // 2], sig, lead)
        bm = _hy_mid_call(a, twb, fblk, fblkt, spec, order)
        sig, lead = _hy_out_call(f_inv, bm, sig, lead, z, order, skip)[None], 0
    return sig.reshape(b, l, ch)


def _layer_weights(args, l):
    return {k: v[l] for k, v in args.items()}


def kernel(x, c, ctx, c_ctx, w_mod, b_mod, g_pre_mix, g_post_mix, g_pre_mlp, g_post_mlp, w_in, w_out, pool_w, pool_scale, q_norm_g, k_norm_g, lru_conv_w, lru_conv_b, lru_wa, lru_ba, lru_wx, lru_bx, lru_lambda, hy_conv_w, hy_conv_b, hy_w1, hy_b1, hy_freq, hy_w2, hy_b2, hy_w3, hy_skip, mlp_w1, mlp_w2):
    b, l, d = x.shape
    lc = ctx.shape[1]
    depth = w_mod.shape[0]
    per_layer = dict(g_pre_mix=g_pre_mix, g_post_mix=g_post_mix, g_pre_mlp=g_pre_mlp, g_post_mlp=g_post_mlp,
                     w_in=w_in.astype(BF16), w_out=w_out.astype(BF16), pool_w=pool_w, pool_scale=pool_scale,
                     q_norm_g=q_norm_g, k_norm_g=k_norm_g, lru_conv_w=lru_conv_w, lru_conv_b=lru_conv_b,
                     lru_wa=lru_wa, lru_ba=lru_ba, lru_wx=lru_wx, lru_bx=lru_bx, lru_lambda=lru_lambda,
                     hy_conv_w=hy_conv_w, hy_conv_b=hy_conv_b, hy_w1=hy_w1, hy_b1=hy_b1, hy_freq=hy_freq,
                     hy_w2=hy_w2, hy_b2=hy_b2, hy_w3=hy_w3, hy_skip=hy_skip,
                     mlp_w1=mlp_w1.astype(BF16), mlp_w2=mlp_w2.astype(BF16))

    cos, sin = _rope_tables(l)
    avg = _block_diag(jnp.full((2, HEAD_DIM, HEAD_DIM), 1.0 / HEAD_DIM, F32)).astype(BF16)
    emb_x, emb_c = _hyena_embedding(l), _hyena_embedding(lc)
    dft_x = _dft_tables(l)
    fs_c, finv_c = _dense_dft_tables(lc)
    deltas = jnp.abs(jnp.linspace(HY_MIN_DECAY, HY_MAX_DECAY, 256, dtype=F32))
    deltas2 = jnp.tile(deltas, 2).reshape(1, 512)

    cond = jnp.zeros((8, d), F32).at[:b].set(c).at[b].set(c_ctx)
    mods = _mod_call(cond, w_mod, b_mod).reshape(depth, 8, 6, d)

    cx = ctx
    for li in range(depth):
        w = _layer_weights(per_layer, li)
        need_ctx = li < depth - 1
        mod_x = mods[li, :b]
        mod_c = jnp.broadcast_to(mods[li, b:b + 1], (b, 6, d))
        w1p = jnp.pad(w["hy_w1"], ((0, HY_EMB_PAD - HY_EMB), (0, 0)))
        pool_bd = _block_diag(w["pool_w"]).astype(BF16)
        gq = jnp.tile(w["q_norm_g"], 2).reshape(1, 128)
        gk = jnp.tile(w["k_norm_g"], 2).reshape(1, 128)
        wa = [_block_diag(w["lru_wa"][dr]).astype(BF16) for dr in range(2)]
        wx = [_block_diag(w["lru_wx"][dr]).astype(BF16) for dr in range(2)]

        px = _inproj_call(x, mod_x, w["g_pre_mix"], w["w_in"])
        pc = _inproj_call(cx, mod_c, w["g_pre_mix"], w["w_in"])

        qtx, kx, vtx = _qkv_call(px, cos, sin, gq, gk, avg)
        qtc, kc, vtc = _qkv_call(pc, None, None, gq, gk, avg)
        yb_x = _flash_call(qtx, kx, vtx, kc, vtc)

        hs_c, hs_x = [], []
        for dr in range(2):
            lw = (w["lru_conv_w"], w["lru_conv_b"], wa[dr], w["lru_ba"][dr], wx[dr], w["lru_bx"][dr],
                  w["lru_lambda"][dr])
            hc = _lru_call(pc, jnp.zeros((b, 1, 256), F32), *lw, reverse=bool(dr))
            h0 = hc[:, :1] if dr else hc[:, -1:]
            hs_c.append(hc)
            hs_x.append(_lru_call(px, h0, *lw, reverse=bool(dr)))

        ya_x = _pool_call(px, pool_bd, w["pool_scale"])
        hy = (w1p, w["hy_b1"], w["hy_freq"], w["hy_w2"], w["hy_b2"], w["hy_w3"], deltas2)
        filt_x, nrm_x = _hy_filter_call(emb_x, *hy, seq_len=l)
        yd_x = _hyena_long(px, w, dft_x, filt_x, nrm_x)

        x = _post_call(x, ya_x, yb_x, hs_x[0], hs_x[1], px, yd_x, mod_x, w["w_out"], w["g_post_mix"],
                       w["g_pre_mlp"], w["g_post_mlp"], w["mlp_w1"], w["mlp_w2"])

        if need_ctx:
            yb_c = _flash_call(qtc, kc, vtc)
            ya_c = _pool_call(pc, pool_bd, w["pool_scale"])
            filt_c, nrm_c = _hy_filter_call(emb_c, *hy, seq_len=lc)
            z_c = _hy_conv_call(pc, w["hy_conv_w"], w["hy_conv_b"])
            yd_c = _hy_small_call(z_c, filt_c, nrm_c, fs_c, finv_c, w["hy_skip"])
            cx = _post_call(cx, ya_c, yb_c, hs_c[0], hs_c[1], pc, yd_c, mod_c, w["w_out"], w["g_post_mix"],
                            w["g_pre_mlp"], w["g_post_mlp"], w["mlp_w1"], w["mlp_w2"])
    return x
```

```python
import functools
import math

import jax
import jax.numpy as jnp
from jax import lax
from jax.experimental import pallas as pl
from jax.experimental.pallas import tpu as pltpu

F32 = jnp.float32
BF16 = jnp.bfloat16
EPS = 1e-6

GRID_W = 64
HEAD_DIM = 64
QUERY_SCALE = HEAD_DIM ** -0.5 * math.log2(math.e)
N_KV_HEADS = 2
ROPE_FREQS = 16
ROPE_THETA = 10000.0
POOL_WINDOWS = (2, 4, 8, 16)
LRU_C = 8.0
HY_EMB = 33
HY_EMB_PAD = 64
HY_BANDS = 16
HY_MIN_DECAY = math.log(1e-2) / 1.5
HY_MAX_DECAY = math.log(1e-2) / 0.3
DFT_N2 = 128
HALO = 16
SLAB = 16
OUTER_FREQS_PER_STEP = 8
UNIT_KEYS = 4096
QUERY_CHAIN = 512
VMEM_LIMIT = 56 * 1024 * 1024
HIGHEST = lax.Precision.HIGHEST


def _params(*sem):
    return pltpu.CompilerParams(dimension_semantics=sem, vmem_limit_bytes=VMEM_LIMIT)


def _resident(shape, index_map):
    return pl.BlockSpec(shape, index_map, pipeline_mode=pl.Buffered(1))


def _rms(x, g):
    return x * lax.rsqrt(jnp.mean(x * x, axis=-1, keepdims=True) + EPS) * g


def _shift_rows(x, k):
    n = x.shape[0]
    return pltpu.roll(x, k % n, 0)


def _with_halo(prev_ref, cur_ref, next_ref, tile, ntiles):
    prev = jnp.where(tile > 0, prev_ref[...], 0.0)
    nxt = jnp.where(tile < ntiles - 1, next_ref[...], 0.0)
    return jnp.concatenate([prev, cur_ref[...], nxt], axis=0)


def _halo_specs(tm, width, col, ntiles, tile_of=lambda i: i, nlead=2):
    r = tm // HALO
    last = ntiles * r - 1

    def prev_map(*g):
        return (g[0], jnp.maximum(tile_of(g[nlead - 1]) * r - 1, 0), col)

    def cur_map(*g):
        return (g[0], tile_of(g[nlead - 1]), col)

    def next_map(*g):
        return (g[0], jnp.minimum((tile_of(g[nlead - 1]) + 1) * r, last), col)

    return [pl.BlockSpec((None, HALO, width), prev_map),
            pl.BlockSpec((None, tm, width), cur_map),
            pl.BlockSpec((None, HALO, width), next_map)]


def _mod_kernel(c_ref, w_ref, b_ref, o_ref):
    cond = c_ref[...]
    act = (cond * jax.nn.sigmoid(cond)).astype(BF16)
    o_ref[...] = jnp.dot(act, w_ref[...].astype(BF16), preferred_element_type=F32) + b_ref[...]


def _mod_call(cond, w_mod, b_mod):
    depth, d, n = w_mod.shape
    tn = 1024
    return pl.pallas_call(
        _mod_kernel,
        grid=(depth, n // tn),
        in_specs=[pl.BlockSpec((8, d), lambda l, j: (0, 0)),
                  pl.BlockSpec((None, d, tn), lambda l, j: (l, 0, j)),
                  pl.BlockSpec((None, 1, tn), lambda l, j: (l, 0, j))],
        out_specs=pl.BlockSpec((None, 8, tn), lambda l, j: (l, 0, j)),
        out_shape=jax.ShapeDtypeStruct((depth, 8, n), F32),
        compiler_params=_params("parallel", "parallel"),
        name="adaln_mod",
    )(cond, w_mod, b_mod.reshape(depth, 1, n))


def _inproj_kernel(x_ref, mod_ref, g_ref, w_ref, o_ref):
    m = mod_ref[...]
    h = _rms(x_ref[...], g_ref[...]) * (1.0 + m[1:2]) + m[0:1]
    o_ref[...] = jnp.dot(h.astype(BF16), w_ref[...], preferred_element_type=F32)


def _inproj_call(x, mod, g, w):
    b, l, d = x.shape
    n = w.shape[1]
    tm = min(l, 1024)
    return pl.pallas_call(
        _inproj_kernel,
        grid=(b, l // tm),
        in_specs=[pl.BlockSpec((None, tm, d), lambda bi, i: (bi, i, 0)),
                  pl.BlockSpec((None, 6, d), lambda bi, i: (bi, 0, 0)),
                  pl.BlockSpec((1, d), lambda bi, i: (0, 0)),
                  _resident((d, n), lambda bi, i: (0, 0))],
        out_specs=pl.BlockSpec((None, tm, n), lambda bi, i: (bi, i, 0)),
        out_shape=jax.ShapeDtypeStruct((b, l, n), F32),
        compiler_params=_params("parallel", "parallel"),
        name="in_proj",
    )(x, mod, g.reshape(1, d), w)


def _pool_kernel(prev_ref, cur_ref, next_ref, w_ref, sc_ref, o_ref, *, seq_len):
    i = pl.program_id(1)
    tm, ch = cur_ref.shape
    e = _with_halo(prev_ref, cur_ref, next_ref, i, pl.num_programs(1))
    s2 = e + _shift_rows(e, 1)
    s4 = _shift_rows(s2, 1) + _shift_rows(s2, -1)
    s8 = _shift_rows(s4, 2) + _shift_rows(s4, -2)
    s16 = _shift_rows(s8, 4) + _shift_rows(s8, -4)
    t = i * tm + lax.broadcasted_iota(jnp.int32, (tm, 1), 0)
    group = lax.broadcasted_iota(jnp.int32, (1, ch), 1) // (ch // len(POOL_WINDOWS))
    mean = jnp.zeros((tm, ch), F32)
    for gi, (win, s) in enumerate(zip(POOL_WINDOWS, (s2, s4, s8, s16))):
        lo = jnp.maximum(t - win // 2, 0)
        hi = jnp.minimum(t + win - win // 2, seq_len)
        m = s[HALO:HALO + tm] / (hi - lo).astype(F32)
        mean = jnp.where(group == gi, m, mean)
    d = (mean - cur_ref[...]).astype(BF16)
    o_ref[...] = jnp.dot(d, w_ref[...], preferred_element_type=F32) * sc_ref[...]


def _pool_call(p, w_bd, scale):
    b, l, _ = p.shape
    ch = w_bd.shape[0]
    tm = min(l, 1024)
    nt = l // tm
    return pl.pallas_call(
        functools.partial(_pool_kernel, seq_len=l),
        grid=(b, nt),
        in_specs=_halo_specs(tm, ch, 0, nt) + [
            pl.BlockSpec((ch, ch), lambda bi, i: (0, 0)),
            pl.BlockSpec((1, ch), lambda bi, i: (0, 0))],
        out_specs=pl.BlockSpec((None, tm, ch), lambda bi, i: (bi, i, 0)),
        out_shape=jax.ShapeDtypeStruct((b, l, ch), F32),
        compiler_params=_params("parallel", "parallel"),
        name="pool_mixer",
    )(p, p, p, w_bd, scale.reshape(1, ch))


def _head_norm(x, g, avg):
    sq = x * x
    hi = sq.astype(BF16)
    lo = (sq - hi.astype(F32)).astype(BF16)
    ms = (jnp.dot(hi, avg, preferred_element_type=F32) + jnp.dot(lo, avg, preferred_element_type=F32))
    return x * lax.rsqrt(ms + EPS) * g


def _rope(x, cos, sin):
    n = x.shape[1]
    lane = lax.broadcasted_iota(jnp.int32, (1, n), 1)
    partner = jnp.where(lane % 32 < 16, pltpu.roll(x, n - 16, 1), pltpu.roll(x, 16, 1))
    return x * cos + partner * sin


def _qkv_kernel(*refs, rope):
    if rope:
        q_ref, kv_ref, cos_ref, sin_ref, gq_ref, gk_ref, avg_ref, qt_ref, ko_ref, vt_ref = refs
    else:
        q_ref, kv_ref, gq_ref, gk_ref, avg_ref, qt_ref, ko_ref, vt_ref = refs
    avg = avg_ref[...]
    for half in range(2):
        q = _head_norm(q_ref[:, half * 128:(half + 1) * 128], gq_ref[...], avg)
        if rope:
            q = _rope(q, cos_ref[...], sin_ref[...])
        qt = (q * QUERY_SCALE).T
        for h in range(2):
            qt_ref[2 * half + h] = qt[h * HEAD_DIM:(h + 1) * HEAD_DIM].astype(BF16)
    k = _head_norm(kv_ref[:, 0:128], gk_ref[...], avg)
    if rope:
        k = _rope(k, cos_ref[...], sin_ref[...])
    ko_ref[...] = k.astype(BF16)
    vt = kv_ref[:, 128:256].T
    for h in range(N_KV_HEADS):
        vt_ref[h] = vt[h * HEAD_DIM:(h + 1) * HEAD_DIM].astype(BF16)


def _qkv_call(p, cos, sin, gq, gk, avg):
    b, l, _ = p.shape
    tm = min(l, 512)
    rope = cos is not None
    tables = [cos, sin] if rope else []
    table_specs = [pl.BlockSpec((tm, 128), lambda bi, i: (i, 0))] * 2 if rope else []
    const = pl.BlockSpec((1, 128), lambda bi, i: (0, 0))
    return pl.pallas_call(
        functools.partial(_qkv_kernel, rope=rope),
        grid=(b, l // tm),
        in_specs=[pl.BlockSpec((None, tm, 256), lambda bi, i: (bi, i, 1)),
                  pl.BlockSpec((None, tm, 256), lambda bi, i: (bi, i, 2))] + table_specs + [
                  const, const, pl.BlockSpec((128, 128), lambda bi, i: (0, 0))],
        out_specs=[pl.BlockSpec((None, 2 * N_KV_HEADS, HEAD_DIM, tm), lambda bi, i: (bi, 0, 0, i)),
                   pl.BlockSpec((None, tm, 128), lambda bi, i: (bi, i, 0)),
                   pl.BlockSpec((None, N_KV_HEADS, None, HEAD_DIM, tm), lambda bi, i: (bi, 0, i, 0, 0))],
        out_shape=[jax.ShapeDtypeStruct((b, 2 * N_KV_HEADS, HEAD_DIM, l), BF16),
                   jax.ShapeDtypeStruct((b, l, 128), BF16),
                   jax.ShapeDtypeStruct((b, N_KV_HEADS, l // tm, HEAD_DIM, tm), BF16)],
        compiler_params=_params("parallel", "parallel"),
        name="qkv_prep",
    )(p, p, *tables, gq, gk, avg)


def _flash_kernel(*refs, with_ctx):
    if with_ctx:
        q_ref, k_ref, vt_ref, kc_ref, vtc_ref, o_ref, qs, m_s, l_s, acc, sb0, sb1 = refs
    else:
        q_ref, k_ref, vt_ref, o_ref, qs, m_s, l_s, acc, sb0, sb1 = refs
    h = pl.program_id(1)
    kj = pl.program_id(3)
    tq = q_ref.shape[-1]
    qc = qs.shape[-1]
    per_head = tq // qc
    nchain = 2 * per_head
    nkc, _, tkc = vt_ref.shape
    cpu = min(nkc, UNIT_KEYS // tkc)
    ukeys = cpu * tkc
    nunits = nchain * (nkc // cpu)

    def sublane_all(op, x):
        for sh in (1, 2, 4):
            x = op(x, pltpu.roll(x, sh, 0))
        return x

    def softmax_pv(s, vts, c):
        s3 = s.reshape(s.shape[0] // 8, 8, qc)
        m_prev = m_s[c]
        m_new = jnp.maximum(m_prev, sublane_all(jnp.maximum, jnp.max(s3, axis=0)))
        alpha = jnp.exp2(m_prev - m_new)
        p3 = jnp.exp2(s3 - m_new[None])
        l_s[c] = alpha * l_s[c] + jnp.sum(p3, axis=0)
        p = p3.reshape(s.shape).astype(BF16)
        step = s.shape[0] // len(vts)
        pv = sum(jnp.dot(vt, p[i * step:(i + 1) * step], preferred_element_type=F32) for i, vt in enumerate(vts))
        acc[c] = (acc[c].reshape(HEAD_DIM // 8, 8, qc) * alpha[None]).reshape(HEAD_DIM, qc) + pv
        m_s[c] = m_new

    def scores(u, buf):
        rows = pl.ds(pl.multiple_of((u // nchain) * ukeys, ukeys), ukeys)
        buf[...] = jnp.dot(k_ref[rows, :], qs[u % nchain], preferred_element_type=F32)

    def consume(u, buf):
        softmax_pv(buf[...], [vt_ref[(u // nchain) * cpu + i] for i in range(cpu)], u % nchain)

    @pl.when(kj == 0)
    def _():
        qs[...] = jnp.zeros(qs.shape, BF16)
        rows = pl.ds(pl.multiple_of(h * HEAD_DIM, HEAD_DIM), HEAD_DIM)
        for c in range(nchain):
            qs[c, rows, :] = q_ref[c // per_head, :, (c % per_head) * qc:(c % per_head + 1) * qc]
        m_s[...] = jnp.full(m_s.shape, -1e30, F32)
        l_s[...] = jnp.zeros(l_s.shape, F32)
        acc[...] = jnp.zeros(acc.shape, F32)
        if with_ctx:
            for c in range(nchain):
                softmax_pv(jnp.dot(kc_ref[...], qs[c], preferred_element_type=F32), [vtc_ref[...]], c)

    scores(0, sb0)

    def pair(j, carry):
        scores(2 * j + 1, sb1)
        consume(2 * j, sb0)
        scores(2 * j + 2, sb0)
        consume(2 * j + 1, sb1)
        return carry

    lax.fori_loop(0, nunits // 2 - 1, pair, 0)
    scores(nunits - 1, sb1)
    consume(nunits - 2, sb0)
    consume(nunits - 1, sb1)

    @pl.when(kj == pl.num_programs(3) - 1)
    def _():
        for r in range(per_head):
            halves = []
            for c in (r, r + per_head):
                l = sublane_all(jnp.add, l_s[c])
                halves.append((acc[c].reshape(HEAD_DIM // 8, 8, qc) / l[None]).reshape(HEAD_DIM, qc))
            o_ref[r * qc:(r + 1) * qc, :] = jnp.concatenate(halves, axis=0).T.astype(o_ref.dtype)


def _flash_call(qt, k, vt, kc=None, vtc=None):
    b, _, _, lq = qt.shape
    lk = k.shape[1]
    tkc = vt.shape[-1]
    tq = min(lq, 1024)
    tk = lk
    nkc = tk // tkc
    qc = min(tq, QUERY_CHAIN)
    nchain = 2 * tq // qc
    ukeys = min(nkc, UNIT_KEYS // tkc) * tkc
    with_ctx = kc is not None
    in_specs = [pl.BlockSpec((None, 2, HEAD_DIM, tq), lambda bi, h, i, j: (bi, h, 0, i)),
                pl.BlockSpec((None, tk, 128), lambda bi, h, i, j: (bi, j, 0)),
                pl.BlockSpec((None, None, nkc, HEAD_DIM, tkc), lambda bi, h, i, j: (bi, h, j, 0, 0))]
    args = [qt, k, vt]
    if with_ctx:
        c = kc.shape[1]
        in_specs += [pl.BlockSpec((None, c, 128), lambda bi, h, i, j: (bi, 0, 0)),
                     pl.BlockSpec((None, None, None, HEAD_DIM, c), lambda bi, h, i, j: (bi, h, 0, 0, 0))]
        args += [kc, vtc]
    return pl.pallas_call(
        functools.partial(_flash_kernel, with_ctx=with_ctx),
        grid=(b, N_KV_HEADS, lq // tq, lk // tk),
        in_specs=in_specs,
        out_specs=pl.BlockSpec((None, tq, 128), lambda bi, h, i, j: (bi, i, h)),
        out_shape=jax.ShapeDtypeStruct((b, lq, 256), BF16),
        scratch_shapes=[pltpu.VMEM((nchain, 2 * HEAD_DIM, qc), BF16),
                        pltpu.VMEM((nchain, 8, qc), F32),
                        pltpu.VMEM((nchain, 8, qc), F32),
                        pltpu.VMEM((nchain, HEAD_DIM, qc), F32),
                        pltpu.VMEM((ukeys, qc), F32),
                        pltpu.VMEM((ukeys, qc), F32)],
        compiler_params=_params("parallel", "parallel", "parallel", "arbitrary"),
        name="flash_attention",
    )(*args)


def _lru_kernel(prev_ref, cur_ref, next_ref, cw_ref, cb_ref, wa_ref, br_ref, wx_ref, bi_ref, lam_ref, h0_ref,
                o_ref, a_s, b_s, carry, *, reverse):
    i = pl.program_id(1)
    nt = pl.num_programs(1)
    tm, ch = cur_ref.shape
    tile = nt - 1 - i if reverse else i

    @pl.when(i == 0)
    def _():
        carry[...] = jnp.broadcast_to(h0_ref[...], carry.shape)

    e = _with_halo(prev_ref, cur_ref, next_ref, tile, nt)
    cw = cw_ref[...]
    xc = cb_ref[...] + sum(_shift_rows(e, 2 - j) * cw[j:j + 1] for j in range(4))
    xc = xc[HALO:HALO + tm]
    xb = xc.astype(BF16)
    r = jax.nn.sigmoid(jnp.dot(xb, wa_ref[...], preferred_element_type=F32) + br_ref[...])
    gate_i = jax.nn.sigmoid(jnp.dot(xb, wx_ref[...], preferred_element_type=F32) + bi_ref[...])
    nl = -lam_ref[...]
    softplus = jnp.maximum(nl, 0.0) + jnp.log1p(jnp.exp(-jnp.abs(nl)))
    a = jnp.exp(-LRU_C * r * softplus)
    bb = jnp.sqrt(1.0 - a * a) * (gate_i * xc)

    row = lax.broadcasted_iota(jnp.int32, (tm, 1), 0) % 8
    for k in (1, 2, 4):
        sh = -k if reverse else k
        take = (row < 8 - k) if reverse else (row >= k)
        bb = jnp.where(take, a * _shift_rows(bb, sh) + bb, bb)
        a = jnp.where(take, a * _shift_rows(a, sh), a)
    a_s[...] = a
    b_s[...] = bb
    ngroups = tm // 8

    def body(g, hprev):
        gg = ngroups - 1 - g if reverse else g
        sl = pl.ds(pl.multiple_of(gg * 8, 8), 8)
        rows = b_s[sl, :] + a_s[sl, :] * hprev
        o_ref[sl, :] = rows
        edge = rows[0:1] if reverse else rows[7:8]
        return jnp.broadcast_to(edge, (8, ch))

    carry[...] = lax.fori_loop(0, ngroups, body, carry[...])


def _lru_call(p, h0, cw, cb, wa, br, wx, bi, lam, reverse):
    b, l, _ = p.shape
    ch = wa.shape[0]
    tm = min(l, 1024)
    nt = l // tm
    tile_of = (lambda i: nt - 1 - i) if reverse else (lambda i: i)
    vec = pl.BlockSpec((1, ch), lambda bi_, i: (0, 0))
    mat = pl.BlockSpec((ch, ch), lambda bi_, i: (0, 0))
    return pl.pallas_call(
        functools.partial(_lru_kernel, reverse=reverse),
        grid=(b, nt),
        in_specs=_halo_specs(tm, ch, 3, nt, tile_of) + [
            pl.BlockSpec((4, ch), lambda bi_, i: (0, 0)), vec, mat, vec, mat, vec, vec,
            pl.BlockSpec((None, 1, ch), lambda bi_, i: (bi_, 0, 0))],
        out_specs=pl.BlockSpec((None, tm, ch), lambda bi_, i: (bi_, tile_of(i), 0)),
        out_shape=jax.ShapeDtypeStruct((b, l, ch), F32),
        scratch_shapes=[pltpu.VMEM((tm, ch), F32), pltpu.VMEM((tm, ch), F32), pltpu.VMEM((8, ch), F32)],
        compiler_params=_params("parallel", "arbitrary"),
        name="rglru_bwd" if reverse else "rglru_fwd",
    )(p, p, p, cw, cb.reshape(1, ch), wa, br.reshape(1, ch), wx, bi.reshape(1, ch), lam.reshape(1, ch), h0)


def _hy_conv_kernel(prev_ref, cur_ref, next_ref, w_ref, b_ref, o_ref):
    i = pl.program_id(2)
    tm = cur_ref.shape[0]
    e = _with_halo(prev_ref, cur_ref, next_ref, i, pl.num_programs(2))
    w = w_ref[...]
    z = b_ref[...] + sum(_shift_rows(e, 1 - j) * w[j:j + 1] for j in range(3))
    o_ref[...] = z[HALO:HALO + tm]


def _hy_conv_call(p, w, bias):
    b, l, _ = p.shape
    ch = 256
    tm = min(l, 2048)
    nt = l // tm
    specs = _halo_specs(tm, ch, 0, nt, nlead=3)
    def with_col(spec):
        inner = spec.index_map
        return pl.BlockSpec(spec.block_shape, lambda bi, j, i: inner(bi, j, i)[:2] + (5 + j,))
    return pl.pallas_call(
        _hy_conv_kernel,
        grid=(b, 3, nt),
        in_specs=[with_col(s) for s in specs] + [
            pl.BlockSpec((3, ch), lambda bi, j, i: (0, j)),
            pl.BlockSpec((1, ch), lambda bi, j, i: (0, j))],
        out_specs=pl.BlockSpec((None, None, tm, ch), lambda bi, j, i: (j, bi, i, 0)),
        out_shape=jax.ShapeDtypeStruct((3, b, l, ch), F32),
        compiler_params=_params("parallel", "parallel", "parallel"),
        name="hyena_short_conv",
    )(p, p, p, w, bias.reshape(1, -1))


def _dot_bf16x3(a, b):
    a_hi = a.astype(BF16)
    b_hi = b.astype(BF16)
    a_lo = (a - a_hi.astype(F32)).astype(BF16)
    b_lo = (b - b_hi.astype(F32)).astype(BF16)
    dot = functools.partial(jnp.dot, preferred_element_type=F32)
    return dot(a_hi, b_hi) + dot(a_lo, b_hi) + dot(a_hi, b_lo)


def _hy_filter_kernel(emb_ref, w1_ref, b1_ref, fq_ref, w2_ref, b2_ref, w3_ref, dl_ref, f_ref, nrm_ref, *, seq_len):
    i = pl.program_id(0)
    tr = emb_ref.shape[0]
    emb = emb_ref[...]
    fq = fq_ref[...]
    h = jnp.sin(fq * (jnp.dot(emb, w1_ref[...], precision=HIGHEST, preferred_element_type=F32) + b1_ref[...]))
    h = jnp.sin(fq * (jnp.dot(h, w2_ref[...], precision=HIGHEST, preferred_element_type=F32) + b2_ref[...]))
    h = _dot_bf16x3(h, w3_ref[...])
    n = i * tr + lax.broadcasted_iota(jnp.int32, (tr, 1), 0)
    filt = jnp.where(n == seq_len, 0.0, h) * jnp.exp(-emb[:, 0:1] * dl_ref[...])
    f_ref[...] = filt

    @pl.when(i == 0)
    def _():
        nrm_ref[...] = jnp.zeros(nrm_ref.shape, F32)

    nrm_ref[...] += jnp.sum(jnp.abs(filt), axis=0, keepdims=True)


def _hy_filter_call(emb_circ, w1, b1, fq, w2, b2, w3, deltas2, seq_len):
    n2l = emb_circ.shape[0]
    hid = w2.shape[0]
    cols = w3.shape[1] // 2
    tr = min(seq_len, 512)
    full = lambda shape: pl.BlockSpec(shape, lambda i: (0, 0))
    return pl.pallas_call(
        functools.partial(_hy_filter_kernel, seq_len=seq_len),
        grid=(n2l // tr,),
        in_specs=[pl.BlockSpec((tr, HY_EMB_PAD), lambda i: (i, 0)),
                  full((HY_EMB_PAD, hid)), full((1, hid)), full((1, hid)),
                  full((hid, hid)), full((1, hid)),
                  pl.BlockSpec((hid, cols), lambda i: (0, (i * tr) // seq_len)),
                  full((1, cols))],
        out_specs=[pl.BlockSpec((tr, cols), lambda i: (i, 0)), full((1, cols))],
        out_shape=[jax.ShapeDtypeStruct((n2l, cols), F32), jax.ShapeDtypeStruct((1, cols), F32)],
        compiler_params=_params("arbitrary"),
        name="hyena_filter",
    )(emb_circ, w1, b1.reshape(1, hid), fq.reshape(1, hid), w2, b2.reshape(1, hid), w3, deltas2)


def _outer_matmuls(fmat, x):
    xt = jnp.swapaxes(x, 0, 1).astype(BF16)
    ys = [jnp.dot(fmat, xt[s], preferred_element_type=F32) for s in range(x.shape[1])]
    return jnp.swapaxes(jnp.stack(ys, axis=0), 0, 1)


def _dft_outer_kernel(f_ref, x_ref, o_ref):
    m = o_ref.shape[1]
    y = _outer_matmuls(f_ref[...], x_ref[...]).astype(BF16)
    o_ref[0] = y[:m]
    o_ref[1] = y[m:]


def _dft_outer_call(fmat, x, lead):
    _, g, k, n2, w = x.shape
    m = fmat.shape[0] // 2
    return pl.pallas_call(
        _dft_outer_kernel,
        grid=(g, n2 // SLAB, w // 256),
        in_specs=[pl.BlockSpec((2 * m, k), lambda gi, j, c: (0, 0)),
                  pl.BlockSpec((None, None, k, SLAB, 256), lambda gi, j, c: (lead, gi, 0, j, c))],
        out_specs=pl.BlockSpec((None, 2, m, SLAB, 256), lambda gi, j, c: (gi, 0, 0, j, c)),
        out_shape=jax.ShapeDtypeStruct((g, 2, m, n2, w), BF16),
        compiler_params=_params("parallel", "parallel", "parallel"),
        name="dft_outer",
    )(fmat, x)


def _cmul(ar, ai, br, bi):
    return ar * br - ai * bi, ar * bi + ai * br


def _twiddled_dft_block(f_ref, tw_ref, k):
    gr, gi = _cmul(f_ref[0], f_ref[1], tw_ref[0, k:k + 1, :], tw_ref[1, k:k + 1, :])
    return jnp.concatenate([jnp.concatenate([gr, -gi], axis=1),
                            jnp.concatenate([gi, gr], axis=1)], axis=0).astype(BF16)


def _hy_spec_kernel(a_ref, tw_ref, f_ref, nrm_ref, h_ref):
    n2 = a_ref.shape[2]
    inv = 1.0 / nrm_ref[...]
    for k in range(a_ref.shape[1]):
        a = jnp.concatenate([a_ref[0, k], a_ref[1, k]], axis=0)
        z = jnp.dot(_twiddled_dft_block(f_ref, tw_ref, k), a, preferred_element_type=F32)
        h_ref[k, 0] = z[:n2] * inv
        h_ref[k, 1] = z[n2:] * inv


def _hy_spec_call(a, tw, fri, nrm):
    _, _, n1, n2, cols = a.shape
    kb = min(n1, OUTER_FREQS_PER_STEP)
    return pl.pallas_call(
        _hy_spec_kernel,
        grid=(n1 // kb,),
        in_specs=[pl.BlockSpec((None, 2, kb, n2, cols), lambda k: (0, 0, k, 0, 0)),
                  pl.BlockSpec((2, kb, n2), lambda k: (0, k, 0)),
                  pl.BlockSpec((2, n2, n2), lambda k: (0, 0, 0)),
                  pl.BlockSpec((1, cols), lambda k: (0, 0))],
        out_specs=pl.BlockSpec((kb, 2, n2, cols), lambda k: (k, 0, 0, 0)),
        out_shape=jax.ShapeDtypeStruct((n1, 2, n2, cols), F32),
        compiler_params=_params("parallel"),
        name="hyena_filter_spectrum",
    )(a, tw, fri, nrm)


def _hy_mid_kernel(a_ref, tw_ref, f_ref, h_ref, o_ref):
    n2 = a_ref.shape[2]
    for k in range(a_ref.shape[1]):
        g = _twiddled_dft_block(f_ref, tw_ref, k)
        a = jnp.concatenate([a_ref[0, k], a_ref[1, k]], axis=0)
        z = jnp.dot(g, a, preferred_element_type=F32)
        yr, yi = _cmul(z[:n2], z[n2:], h_ref[k, 0], h_ref[k, 1])
        y = jnp.concatenate([yr, yi], axis=0).astype(BF16)
        bk = lax.dot_general(g, y, (((0,), (0,)), ((), ())), preferred_element_type=F32)
        o_ref[0, k] = bk[:n2].astype(BF16)
        o_ref[1, k] = bk[n2:].astype(BF16)


def _hy_mid_call(a, tw, fri, spec, order):
    b, _, n1, n2, ch = a.shape
    kb = min(n1, OUTER_FREQS_PER_STEP)
    return pl.pallas_call(
        _hy_mid_kernel,
        grid=(b, n1 // kb),
        in_specs=[pl.BlockSpec((None, 2, kb, n2, ch), lambda bi, k: (bi, 0, k, 0, 0)),
                  pl.BlockSpec((2, kb, n2), lambda bi, k: (0, k, 0)),
                  pl.BlockSpec((2, n2, n2), lambda bi, k: (0, 0, 0)),
                  pl.BlockSpec((kb, 2, n2, ch), lambda bi, k: (k, 0, 0, order))],
        out_specs=pl.BlockSpec((None, 2, kb, n2, ch), lambda bi, k: (bi, 0, k, 0, 0)),
        out_shape=jax.ShapeDtypeStruct(a.shape, BF16),
        compiler_params=_params("parallel", "parallel"),
        name="hyena_spectral_mid",
    )(a, tw, fri, spec)


def _hy_out_kernel(f_ref, b_ref, u_ref, g_ref, sk_ref, o_ref):
    _, n1, s, ch = b_ref.shape
    y = _outer_matmuls(f_ref[...], b_ref[...].astype(F32).reshape(2 * n1, s, ch))
    o_ref[...] = g_ref[...] * (y + sk_ref[...] * u_ref[...])


def _hy_out_call(finv, bm, u, lead, z, order, skip):
    b, _, n1, n2, ch = bm.shape
    m = finv.shape[0]
    return pl.pallas_call(
        _hy_out_kernel,
        grid=(b, n2 // SLAB),
        in_specs=[pl.BlockSpec((m, 2 * n1), lambda bi, j: (0, 0)),
                  pl.BlockSpec((None, 2, n1, SLAB, ch), lambda bi, j: (bi, 0, 0, j, 0)),
                  pl.BlockSpec((None, None, m, SLAB, ch), lambda bi, j: (lead, bi, 0, j, 0)),
                  pl.BlockSpec((None, None, m, SLAB, ch), lambda bi, j: (order, bi, 0, j, 0)),
                  pl.BlockSpec((None, 1, ch), lambda bi, j: (order, 0, 0))],
        out_specs=pl.BlockSpec((None, m, SLAB, ch), lambda bi, j: (bi, 0, j, 0)),
        out_shape=jax.ShapeDtypeStruct((b, m, n2, ch), F32),
        compiler_params=_params("parallel", "parallel"),
        name="hyena_dft_out",
    )(finv, bm, u, z, skip)


def _hy_small_kernel(z_ref, filt_ref, nrm_ref, fs_ref, finv_ref, sk_ref, o_ref):
    l = z_ref.shape[1]
    ch = z_ref.shape[2]
    fs = fs_ref[...]
    n = fs.shape[0] // 2
    spec = jnp.dot(fs, filt_ref[...], precision=HIGHEST, preferred_element_type=F32) / nrm_ref[...]
    sig = z_ref[2]
    for order in range(2):
        u = jnp.dot(fs[:, :l], sig, precision=HIGHEST, preferred_element_type=F32)
        hr = spec[:n, order * ch:(order + 1) * ch]
        hi = spec[n:, order * ch:(order + 1) * ch]
        yr, yi = _cmul(u[:n], u[n:], hr, hi)
        y = jnp.dot(finv_ref[...], jnp.concatenate([yr, yi], axis=0), precision=HIGHEST, preferred_element_type=F32)
        sig = z_ref[order] * (y + sk_ref[order:order + 1] * sig)
    o_ref[...] = sig


def _hy_small_call(z, filt, nrm, fs, finv, skip):
    _, b, l, ch = z.shape
    full = lambda a: pl.BlockSpec(a.shape, lambda bi: (0,) * a.ndim)
    return pl.pallas_call(
        _hy_small_kernel,
        grid=(b,),
        in_specs=[pl.BlockSpec((3, None, l, ch), lambda bi: (0, bi, 0, 0)),
                  full(filt), full(nrm), full(fs), full(finv), full(skip)],
        out_specs=pl.BlockSpec((None, l, ch), lambda bi: (bi, 0, 0)),
        out_shape=jax.ShapeDtypeStruct((b, l, ch), F32),
        compiler_params=_params("parallel"),
        name="hyena_short_sequence",
    )(z, filt, nrm, fs, finv, skip)


def _post_kernel(x_ref, ya_ref, yb_ref, hf_ref, hb_ref, gt_ref, yd_ref, mod_ref, wo_ref, g1_ref, g2_ref, g3_ref,
                 w1_ref, w2_ref, o_ref):
    m = mod_ref[...]
    gt = gt_ref[...]
    gelu = 0.5 * gt * (1.0 + jnp.tanh(math.sqrt(2.0 / math.pi) * (gt + 0.044715 * gt * gt * gt)))
    yc = (hf_ref[...] + hb_ref[...]) * gelu
    cat = jnp.concatenate([ya_ref[...].astype(BF16), yb_ref[...], yc.astype(BF16), yd_ref[...].astype(BF16)], axis=-1)
    mix = jnp.dot(cat, wo_ref[...], preferred_element_type=F32)
    x1 = x_ref[...] + m[2:3] * _rms(mix, g1_ref[...])
    h = (_rms(x1, g2_ref[...]) * (1.0 + m[4:5]) + m[3:4]).astype(BF16)
    dff = w1_ref.shape[1]
    fc = 1024
    acc = jnp.zeros(x1.shape, F32)
    for c in range(dff // fc):
        a = jnp.maximum(jnp.dot(h, w1_ref[:, c * fc:(c + 1) * fc], preferred_element_type=F32), 0.0)
        acc = acc + jnp.dot((a * a).astype(BF16), w2_ref[c * fc:(c + 1) * fc, :], preferred_element_type=F32)
    o_ref[...] = x1 + m[5:6] * _rms(acc, g3_ref[...])


def _post_call(x, ya, yb, hf, hb, p, yd, mod, wo, g1, g2, g3, w1, w2):
    b, l, d = x.shape
    ch = ya.shape[-1]
    tm = min(l, 512)
    row = lambda w, col=0: pl.BlockSpec((None, tm, w), lambda bi, i: (bi, i, col))
    vec = pl.BlockSpec((1, d), lambda bi, i: (0, 0))
    return pl.pallas_call(
        _post_kernel,
        grid=(b, l // tm),
        in_specs=[row(d), row(ch), row(ch), row(ch), row(ch), row(ch, 4), row(ch),
                  pl.BlockSpec((None, 6, d), lambda bi, i: (bi, 0, 0)),
                  _resident(wo.shape, lambda bi, i: (0, 0)), vec, vec, vec,
                  _resident(w1.shape, lambda bi, i: (0, 0)),
                  _resident(w2.shape, lambda bi, i: (0, 0))],
        out_specs=row(d),
        out_shape=jax.ShapeDtypeStruct((b, l, d), F32),
        compiler_params=_params("parallel", "parallel"),
        name="out_proj_mlp",
    )(x, ya, yb, hf, hb, p, yd, mod, wo, g1.reshape(1, d), g2.reshape(1, d), g3.reshape(1, d), w1, w2)


def _rope_tables(seq_len):
    t = jnp.arange(seq_len)
    inv = ROPE_THETA ** (-jnp.arange(ROPE_FREQS, dtype=F32) / ROPE_FREQS)
    ang_r = (t // GRID_W).astype(F32)[:, None] * inv
    ang_c = (t % GRID_W).astype(F32)[:, None] * inv
    cos = jnp.concatenate([jnp.cos(ang_r)] * 2 + [jnp.cos(ang_c)] * 2, axis=1)
    sin = jnp.concatenate([-jnp.sin(ang_r), jnp.sin(ang_r), -jnp.sin(ang_c), jnp.sin(ang_c)], axis=1)
    return jnp.tile(cos, (1, 2)), jnp.tile(sin, (1, 2))


def _hyena_embedding(seq_len):
    def circular(v):
        return jnp.concatenate([v, v[:1], v[:0:-1]])

    t01 = circular(jnp.linspace(0.0, 1.0, seq_len, dtype=F32))[:, None]
    w = circular(2.0 * math.pi * jnp.arange(seq_len, dtype=F32) / seq_len)
    bands = jnp.linspace(1e-4, HY_BANDS - 1, HY_BANDS, dtype=F32)
    z = w[:, None] * bands[None, :]
    pad = jnp.zeros((2 * seq_len, HY_EMB_PAD - HY_EMB), F32)
    return jnp.concatenate([t01, jnp.cos(z), -jnp.sin(z), pad], axis=-1)


def _dft_cos_sin(rows, cols, n):
    ang = (2.0 * math.pi / n) * ((jnp.arange(rows)[:, None] * jnp.arange(cols)[None, :]) % n).astype(F32)
    return jnp.cos(ang), -jnp.sin(ang)


def _dft_tables(seq_len):
    n = 2 * seq_len
    n2 = DFT_N2
    n1 = n // n2
    fr, fi = _dft_cos_sin(n1, n1, n1)
    f_outer = jnp.concatenate([fr, fi], axis=0).astype(BF16)
    f_inv = (jnp.concatenate([fr, fi], axis=1)[:n1 // 2] * (1.0 / n)).astype(BF16)
    f_inner = jnp.stack(_dft_cos_sin(n2, n2, n2))
    twiddle = jnp.stack(_dft_cos_sin(n1, n2, n))
    return f_outer, f_inv, f_inner, twiddle


def _dense_dft_tables(seq_len):
    n = 2 * seq_len
    fr, fi = _dft_cos_sin(n, n, n)
    fs = jnp.concatenate([fr, fi], axis=0)
    finv = jnp.concatenate([fr, fi], axis=1)[:seq_len] * (1.0 / n)
    return fs, finv


def _block_diag(w):
    g, c, _ = w.shape
    eye = jnp.eye(g, dtype=w.dtype)
    return (eye[:, None, :, None] * w[:, :, None, :]).reshape(g * c, g * c)


def _hyena_long(p, l_w, tables, filt, nrm):
    b, l, _ = p.shape
    f_outer, f_inv, f_inner, twiddle = tables
    n2 = DFT_N2
    n1 = 2 * l // n2
    ch = 256
    z = _hy_conv_call(p, l_w["hy_conv_w"], l_w["hy_conv_b"]).reshape(3, b, n1 // 2, n2, ch)
    fa = _dft_outer_call(f_outer, filt.reshape(1, 1, n1, n2, 2 * ch), 0)
    spec = _hy_spec_call(fa, twiddle, f_inner, nrm)
    skip = l_w["hy_skip"].reshape(2, 1, ch)
    sig, lead = z, 2
    for order in range(2):
        a = _dft_outer_call(f_outer[:, :n1 // 2], sig, lead)
        bm = _hy_mid_call(a, twiddle, f_inner, spec, order)
        sig, lead = _hy_out_call(f_inv, bm, sig, lead, z, order, skip)[None], 0
    return sig.reshape(b, l, ch)


def _layer_weights(args, l):
    return {k: v[l] for k, v in args.items()}


def kernel(x, c, ctx, c_ctx, w_mod, b_mod, g_pre_mix, g_post_mix, g_pre_mlp, g_post_mlp, w_in, w_out, pool_w, pool_scale, q_norm_g, k_norm_g, lru_conv_w, lru_conv_b, lru_wa, lru_ba, lru_wx, lru_bx, lru_lambda, hy_conv_w, hy_conv_b, hy_w1, hy_b1, hy_freq, hy_w2, hy_b2, hy_w3, hy_skip, mlp_w1, mlp_w2):
    b, l, d = x.shape
    lc = ctx.shape[1]
    depth = w_mod.shape[0]
    per_layer = dict(g_pre_mix=g_pre_mix, g_post_mix=g_post_mix, g_pre_mlp=g_pre_mlp, g_post_mlp=g_post_mlp,
                     w_in=w_in.astype(BF16), w_out=w_out.astype(BF16), pool_w=pool_w, pool_scale=pool_scale,
                     q_norm_g=q_norm_g, k_norm_g=k_norm_g, lru_conv_w=lru_conv_w, lru_conv_b=lru_conv_b,
                     lru_wa=lru_wa, lru_ba=lru_ba, lru_wx=lru_wx, lru_bx=lru_bx, lru_lambda=lru_lambda,
                     hy_conv_w=hy_conv_w, hy_conv_b=hy_conv_b, hy_w1=hy_w1, hy_b1=hy_b1, hy_freq=hy_freq,
                     hy_w2=hy_w2, hy_b2=hy_b2, hy_w3=hy_w3, hy_skip=hy_skip,
                     mlp_w1=mlp_w1.astype(BF16), mlp_w2=mlp_w2.astype(BF16))

    cos, sin = _rope_tables(l)
    avg = _block_diag(jnp.full((2, HEAD_DIM, HEAD_DIM), 1.0 / HEAD_DIM, F32)).astype(BF16)
    emb_x, emb_c = _hyena_embedding(l), _hyena_embedding(lc)
    dft_x = _dft_tables(l)
    fs_c, finv_c = _dense_dft_tables(lc)
    deltas = jnp.abs(jnp.linspace(HY_MIN_DECAY, HY_MAX_DECAY, 256, dtype=F32))
    deltas2 = jnp.tile(deltas, 2).reshape(1, 512)

    cond = jnp.zeros((8, d), F32).at[:b].set(c).at[b].set(c_ctx)
    mods = _mod_call(cond, w_mod, b_mod).reshape(depth, 8, 6, d)

    cx = ctx
    for li in range(depth):
        w = _layer_weights(per_layer, li)
        need_ctx = li < depth - 1
        mod_x = mods[li, :b]
        mod_c = jnp.broadcast_to(mods[li, b:b + 1], (b, 6, d))
        w1p = jnp.pad(w["hy_w1"], ((0, HY_EMB_PAD - HY_EMB), (0, 0)))
        pool_bd = _block_diag(w["pool_w"]).astype(BF16)
        gq = jnp.tile(w["q_norm_g"], 2).reshape(1, 128)
        gk = jnp.tile(w["k_norm_g"], 2).reshape(1, 128)
        wa = [_block_diag(w["lru_wa"][dr]).astype(BF16) for dr in range(2)]
        wx = [_block_diag(w["lru_wx"][dr]).astype(BF16) for dr in range(2)]

        px = _inproj_call(x, mod_x, w["g_pre_mix"], w["w_in"])
        pc = _inproj_call(cx, mod_c, w["g_pre_mix"], w["w_in"])

        qtx, kx, vtx = _qkv_call(px, cos, sin, gq, gk, avg)
        qtc, kc, vtc = _qkv_call(pc, None, None, gq, gk, avg)
        yb_x = _flash_call(qtx, kx, vtx, kc, vtc)

        hs_c, hs_x = [], []
        for dr in range(2):
            lw = (w["lru_conv_w"], w["lru_conv_b"], wa[dr], w["lru_ba"][dr], wx[dr], w["lru_bx"][dr],
                  w["lru_lambda"][dr])
            hc = _lru_call(pc, jnp.zeros((b, 1, 256), F32), *lw, reverse=bool(dr))
            h0 = hc[:, :1] if dr else hc[:, -1:]
            hs_c.append(hc)
            hs_x.append(_lru_call(px, h0, *lw, reverse=bool(dr)))

        ya_x = _pool_call(px, pool_bd, w["pool_scale"])
        hy = (w1p, w["hy_b1"], w["hy_freq"], w["hy_w2"], w["hy_b2"], w["hy_w3"], deltas2)
        filt_x, nrm_x = _hy_filter_call(emb_x, *hy, seq_len=l)
        yd_x = _hyena_long(px, w, dft_x, filt_x, nrm_x)

        x = _post_call(x, ya_x, yb_x, hs_x[0], hs_x[1], px, yd_x, mod_x, w["w_out"], w["g_post_mix"],
                       w["g_pre_mlp"], w["g_post_mlp"], w["mlp_w1"], w["mlp_w2"])

        if need_ctx:
            yb_c = _flash_call(qtc, kc, vtc)
            ya_c = _pool_call(pc, pool_bd, w["pool_scale"])
            filt_c, nrm_c = _hy_filter_call(emb_c, *hy, seq_len=lc)
            z_c = _hy_conv_call(pc, w["hy_conv_w"], w["hy_conv_b"])
            yd_c = _hy_small_call(z_c, filt_c, nrm_c, fs_c, finv_c, w["hy_skip"])
            cx = _post_call(cx, ya_c, yb_c, hs_c[0], hs_c[1], pc, yd_c, mod_c, w["w_out"], w["g_post_mix"],
                            w["g_pre_mlp"], w["g_post_mlp"], w["mlp_w1"], w["mlp_w2"])
    return x
```

```python
import functools
import math

import jax
import jax.numpy as jnp
from jax import lax
from jax.experimental import pallas as pl
from jax.experimental.pallas import tpu as pltpu

F32 = jnp.float32
BF16 = jnp.bfloat16
EPS = 1e-6

GRID_W = 64
HEAD_DIM = 64
QUERY_SCALE = HEAD_DIM ** -0.5 * math.log2(math.e)
N_KV_HEADS = 2
ROPE_FREQS = 16
ROPE_THETA = 10000.0
POOL_WINDOWS = (2, 4, 8, 16)
LRU_C = 8.0
HY_EMB = 33
HY_EMB_PAD = 64
HY_BANDS = 16
HY_MIN_DECAY = math.log(1e-2) / 1.5
HY_MAX_DECAY = math.log(1e-2) / 0.3
DFT_N2 = 128
HALO = 16
SLAB = 16
OUTER_FREQS_PER_STEP = 8
UNIT_KEYS = 4096
QUERY_CHAIN = 512
VMEM_LIMIT = 56 * 1024 * 1024
HIGHEST = lax.Precision.HIGHEST


def _params(*sem):
    return pltpu.CompilerParams(dimension_semantics=sem, vmem_limit_bytes=VMEM_LIMIT)


def _resident(shape, index_map):
    return pl.BlockSpec(shape, index_map, pipeline_mode=pl.Buffered(1))


def _rms(x, g):
    return x * lax.rsqrt(jnp.mean(x * x, axis=-1, keepdims=True) + EPS) * g


def _shift_rows(x, k):
    n = x.shape[0]
    return pltpu.roll(x, k % n, 0)


def _with_halo(prev_ref, cur_ref, next_ref, tile, ntiles):
    prev = jnp.where(tile > 0, prev_ref[...], 0.0)
    nxt = jnp.where(tile < ntiles - 1, next_ref[...], 0.0)
    return jnp.concatenate([prev, cur_ref[...], nxt], axis=0)


def _halo_specs(tm, width, col, ntiles, tile_of=lambda i: i, nlead=2):
    r = tm // HALO
    last = ntiles * r - 1

    def prev_map(*g):
        return (g[0], jnp.maximum(tile_of(g[nlead - 1]) * r - 1, 0), col)

    def cur_map(*g):
        return (g[0], tile_of(g[nlead - 1]), col)

    def next_map(*g):
        return (g[0], jnp.minimum((tile_of(g[nlead - 1]) + 1) * r, last), col)

    return [pl.BlockSpec((None, HALO, width), prev_map),
            pl.BlockSpec((None, tm, width), cur_map),
            pl.BlockSpec((None, HALO, width), next_map)]


def _mod_kernel(c_ref, w_ref, b_ref, o_ref):
    cond = c_ref[...]
    act = (cond * jax.nn.sigmoid(cond)).astype(BF16)
    o_ref[...] = jnp.dot(act, w_ref[...].astype(BF16), preferred_element_type=F32) + b_ref[...]


def _mod_call(cond, w_mod, b_mod):
    depth, d, n = w_mod.shape
    tn = 1024
    return pl.pallas_call(
        _mod_kernel,
        grid=(depth, n // tn),
        in_specs=[pl.BlockSpec((8, d), lambda l, j: (0, 0)),
                  pl.BlockSpec((None, d, tn), lambda l, j: (l, 0, j)),
                  pl.BlockSpec((None, 1, tn), lambda l, j: (l, 0, j))],
        out_specs=pl.BlockSpec((None, 8, tn), lambda l, j: (l, 0, j)),
        out_shape=jax.ShapeDtypeStruct((depth, 8, n), F32),
        compiler_params=_params("parallel", "parallel"),
        name="adaln_mod",
    )(cond, w_mod, b_mod.reshape(depth, 1, n))


def _inproj_kernel(x_ref, mod_ref, g_ref, w_ref, o_ref):
    m = mod_ref[...]
    h = _rms(x_ref[...], g_ref[...]) * (1.0 + m[1:2]) + m[0:1]
    o_ref[...] = jnp.dot(h.astype(BF16), w_ref[...], preferred_element_type=F32)


def _inproj_call(x, mod, g, w):
    b, l, d = x.shape
    n = w.shape[1]
    tm = min(l, 1024)
    return pl.pallas_call(
        _inproj_kernel,
        grid=(b, l // tm),
        in_specs=[pl.BlockSpec((None, tm, d), lambda bi, i: (bi, i, 0)),
                  pl.BlockSpec((None, 6, d), lambda bi, i: (bi, 0, 0)),
                  pl.BlockSpec((1, d), lambda bi, i: (0, 0)),
                  _resident((d, n), lambda bi, i: (0, 0))],
        out_specs=pl.BlockSpec((None, tm, n), lambda bi, i: (bi, i, 0)),
        out_shape=jax.ShapeDtypeStruct((b, l, n), F32),
        compiler_params=_params("parallel", "parallel"),
        name="in_proj",
    )(x, mod, g.reshape(1, d), w)


def _pool_kernel(prev_ref, cur_ref, next_ref, w_ref, sc_ref, o_ref, *, seq_len):
    i = pl.program_id(1)
    tm, ch = cur_ref.shape
    e = _with_halo(prev_ref, cur_ref, next_ref, i, pl.num_programs(1))
    s2 = e + _shift_rows(e, 1)
    s4 = _shift_rows(s2, 1) + _shift_rows(s2, -1)
    s8 = _shift_rows(s4, 2) + _shift_rows(s4, -2)
    s16 = _shift_rows(s8, 4) + _shift_rows(s8, -4)
    t = i * tm + lax.broadcasted_iota(jnp.int32, (tm, 1), 0)
    group = lax.broadcasted_iota(jnp.int32, (1, ch), 1) // (ch // len(POOL_WINDOWS))
    mean = jnp.zeros((tm, ch), F32)
    for gi, (win, s) in enumerate(zip(POOL_WINDOWS, (s2, s4, s8, s16))):
        lo = jnp.maximum(t - win // 2, 0)
        hi = jnp.minimum(t + win - win // 2, seq_len)
        m = s[HALO:HALO + tm] / (hi - lo).astype(F32)
        mean = jnp.where(group == gi, m, mean)
    d = (mean - cur_ref[...]).astype(BF16)
    o_ref[...] = jnp.dot(d, w_ref[...], preferred_element_type=F32) * sc_ref[...]


def _pool_call(p, w_bd, scale):
    b, l, _ = p.shape
    ch = w_bd.shape[0]
    tm = min(l, 1024)
    nt = l // tm
    return pl.pallas_call(
        functools.partial(_pool_kernel, seq_len=l),
        grid=(b, nt),
        in_specs=_halo_specs(tm, ch, 0, nt) + [
            pl.BlockSpec((ch, ch), lambda bi, i: (0, 0)),
            pl.BlockSpec((1, ch), lambda bi, i: (0, 0))],
        out_specs=pl.BlockSpec((None, tm, ch), lambda bi, i: (bi, i, 0)),
        out_shape=jax.ShapeDtypeStruct((b, l, ch), F32),
        compiler_params=_params("parallel", "parallel"),
        name="pool_mixer",
    )(p, p, p, w_bd, scale.reshape(1, ch))


def _head_norm(x, g, avg):
    sq = x * x
    hi = sq.astype(BF16)
    lo = (sq - hi.astype(F32)).astype(BF16)
    ms = (jnp.dot(hi, avg, preferred_element_type=F32) + jnp.dot(lo, avg, preferred_element_type=F32))
    return x * lax.rsqrt(ms + EPS) * g


def _rope(x, cos, sin):
    n = x.shape[1]
    lane = lax.broadcasted_iota(jnp.int32, (1, n), 1)
    partner = jnp.where(lane % 32 < 16, pltpu.roll(x, n - 16, 1), pltpu.roll(x, 16, 1))
    return x * cos + partner * sin


def _qkv_kernel(*refs, rope):
    if rope:
        q_ref, kv_ref, cos_ref, sin_ref, gq_ref, gk_ref, avg_ref, qt_ref, ko_ref, vt_ref = refs
    else:
        q_ref, kv_ref, gq_ref, gk_ref, avg_ref, qt_ref, ko_ref, vt_ref = refs
    avg = avg_ref[...]
    for half in range(2):
        q = _head_norm(q_ref[:, half * 128:(half + 1) * 128], gq_ref[...], avg)
        if rope:
            q = _rope(q, cos_ref[...], sin_ref[...])
        qt = (q * QUERY_SCALE).T
        for h in range(2):
            qt_ref[2 * half + h] = qt[h * HEAD_DIM:(h + 1) * HEAD_DIM].astype(BF16)
    k = _head_norm(kv_ref[:, 0:128], gk_ref[...], avg)
    if rope:
        k = _rope(k, cos_ref[...], sin_ref[...])
    ko_ref[...] = k.astype(BF16)
    vt = kv_ref[:, 128:256].T
    for h in range(N_KV_HEADS):
        vt_ref[h] = vt[h * HEAD_DIM:(h + 1) * HEAD_DIM].astype(BF16)


def _qkv_call(p, cos, sin, gq, gk, avg):
    b, l, _ = p.shape
    tm = min(l, 512)
    rope = cos is not None
    tables = [cos, sin] if rope else []
    table_specs = [pl.BlockSpec((tm, 128), lambda bi, i: (i, 0))] * 2 if rope else []
    const = pl.BlockSpec((1, 128), lambda bi, i: (0, 0))
    return pl.pallas_call(
        functools.partial(_qkv_kernel, rope=rope),
        grid=(b, l // tm),
        in_specs=[pl.BlockSpec((None, tm, 256), lambda bi, i: (bi, i, 1)),
                  pl.BlockSpec((None, tm, 256), lambda bi, i: (bi, i, 2))] + table_specs + [
                  const, const, pl.BlockSpec((128, 128), lambda bi, i: (0, 0))],
        out_specs=[pl.BlockSpec((None, 2 * N_KV_HEADS, HEAD_DIM, tm), lambda bi, i: (bi, 0, 0, i)),
                   pl.BlockSpec((None, tm, 128), lambda bi, i: (bi, i, 0)),
                   pl.BlockSpec((None, N_KV_HEADS, None, HEAD_DIM, tm), lambda bi, i: (bi, 0, i, 0, 0))],
        out_shape=[jax.ShapeDtypeStruct((b, 2 * N_KV_HEADS, HEAD_DIM, l), BF16),
                   jax.ShapeDtypeStruct((b, l, 128), BF16),
                   jax.ShapeDtypeStruct((b, N_KV_HEADS, l // tm, HEAD_DIM, tm), BF16)],
        compiler_params=_params("parallel", "parallel"),
        name="qkv_prep",
    )(p, p, *tables, gq, gk, avg)


def _flash_kernel(*refs, with_ctx):
    if with_ctx:
        q_ref, k_ref, vt_ref, kc_ref, vtc_ref, o_ref, qs, m_s, l_s, acc, sb0, sb1 = refs
    else:
        q_ref, k_ref, vt_ref, o_ref, qs, m_s, l_s, acc, sb0, sb1 = refs
    h = pl.program_id(1)
    kj = pl.program_id(3)
    tq = q_ref.shape[-1]
    qc = qs.shape[-1]
    per_head = tq // qc
    nchain = 2 * per_head
    nkc, _, tkc = vt_ref.shape
    cpu = min(nkc, UNIT_KEYS // tkc)
    ukeys = cpu * tkc
    nunits = nchain * (nkc // cpu)

    def sublane_all(op, x):
        for sh in (1, 2, 4):
            x = op(x, pltpu.roll(x, sh, 0))
        return x

    def softmax_pv(s, vts, c):
        s3 = s.reshape(s.shape[0] // 8, 8, qc)
        m_prev = m_s[c]
        m_new = jnp.maximum(m_prev, sublane_all(jnp.maximum, jnp.max(s3, axis=0)))
        alpha = jnp.exp2(m_prev - m_new)
        p3 = jnp.exp2(s3 - m_new[None])
        l_s[c] = alpha * l_s[c] + jnp.sum(p3, axis=0)
        p = p3.reshape(s.shape).astype(BF16)
        step = s.shape[0] // len(vts)
        pv = sum(jnp.dot(vt, p[i * step:(i + 1) * step], preferred_element_type=F32) for i, vt in enumerate(vts))
        acc[c] = (acc[c].reshape(HEAD_DIM // 8, 8, qc) * alpha[None]).reshape(HEAD_DIM, qc) + pv
        m_s[c] = m_new

    def scores(u, buf):
        rows = pl.ds(pl.multiple_of((u // nchain) * ukeys, ukeys), ukeys)
        buf[...] = jnp.dot(k_ref[rows, :], qs[u % nchain], preferred_element_type=F32)

    def consume(u, buf):
        softmax_pv(buf[...], [vt_ref[(u // nchain) * cpu + i] for i in range(cpu)], u % nchain)

    @pl.when(kj == 0)
    def _():
        qs[...] = jnp.zeros(qs.shape, BF16)
        rows = pl.ds(pl.multiple_of(h * HEAD_DIM, HEAD_DIM), HEAD_DIM)
        for c in range(nchain):
            qs[c, rows, :] = q_ref[c // per_head, :, (c % per_head) * qc:(c % per_head + 1) * qc]
        m_s[...] = jnp.full(m_s.shape, -1e30, F32)
        l_s[...] = jnp.zeros(l_s.shape, F32)
        acc[...] = jnp.zeros(acc.shape, F32)
        if with_ctx:
            for c in range(nchain):
                softmax_pv(jnp.dot(kc_ref[...], qs[c], preferred_element_type=F32), [vtc_ref[...]], c)

    scores(0, sb0)

    def pair(j, carry):
        scores(2 * j + 1, sb1)
        consume(2 * j, sb0)
        scores(2 * j + 2, sb0)
        consume(2 * j + 1, sb1)
        return carry

    lax.fori_loop(0, nunits // 2 - 1, pair, 0)
    scores(nunits - 1, sb1)
    consume(nunits - 2, sb0)
    consume(nunits - 1, sb1)

    @pl.when(kj == pl.num_programs(3) - 1)
    def _():
        for r in range(per_head):
            halves = []
            for c in (r, r + per_head):
                l = sublane_all(jnp.add, l_s[c])
                halves.append((acc[c].reshape(HEAD_DIM // 8, 8, qc) / l[None]).reshape(HEAD_DIM, qc))
            o_ref[r * qc:(r + 1) * qc, :] = jnp.concatenate(halves, axis=0).T.astype(o_ref.dtype)


def _flash_call(qt, k, vt, kc=None, vtc=None):
    b, _, _, lq = qt.shape
    lk = k.shape[1]
    tkc = vt.shape[-1]
    tq = min(lq, 1024)
    tk = lk
    nkc = tk // tkc
    qc = min(tq, QUERY_CHAIN)
    nchain = 2 * tq // qc
    ukeys = min(nkc, UNIT_KEYS // tkc) * tkc
    with_ctx = kc is not None
    in_specs = [pl.BlockSpec((None, 2, HEAD_DIM, tq), lambda bi, h, i, j: (bi, h, 0, i)),
                pl.BlockSpec((None, tk, 128), lambda bi, h, i, j: (bi, j, 0)),
                pl.BlockSpec((None, None, nkc, HEAD_DIM, tkc), lambda bi, h, i, j: (bi, h, j, 0, 0))]
    args = [qt, k, vt]
    if with_ctx:
        c = kc.shape[1]
        in_specs += [pl.BlockSpec((None, c, 128), lambda bi, h, i, j: (bi, 0, 0)),
                     pl.BlockSpec((None, None, None, HEAD_DIM, c), lambda bi, h, i, j: (bi, h, 0, 0, 0))]
        args += [kc, vtc]
    return pl.pallas_call(
        functools.partial(_flash_kernel, with_ctx=with_ctx),
        grid=(b, N_KV_HEADS, lq // tq, lk // tk),
        in_specs=in_specs,
        out_specs=pl.BlockSpec((None, tq, 128), lambda bi, h, i, j: (bi, i, h)),
        out_shape=jax.ShapeDtypeStruct((b, lq, 256), BF16),
        scratch_shapes=[pltpu.VMEM((nchain, 2 * HEAD_DIM, qc), BF16),
                        pltpu.VMEM((nchain, 8, qc), F32),
                        pltpu.VMEM((nchain, 8, qc), F32),
                        pltpu.VMEM((nchain, HEAD_DIM, qc), F32),
                        pltpu.VMEM((ukeys, qc), F32),
                        pltpu.VMEM((ukeys, qc), F32)],
        compiler_params=_params("parallel", "parallel", "parallel", "arbitrary"),
        name="flash_attention",
    )(*args)


def _lru_kernel(prev_ref, cur_ref, next_ref, cw_ref, cb_ref, wa_ref, br_ref, wx_ref, bi_ref, lam_ref, h0_ref,
                o_ref, a_s, b_s, carry, *, reverse):
    i = pl.program_id(1)
    nt = pl.num_programs(1)
    tm, ch = cur_ref.shape
    tile = nt - 1 - i if reverse else i

    @pl.when(i == 0)
    def _():
        carry[...] = jnp.broadcast_to(h0_ref[...], carry.shape)

    e = _with_halo(prev_ref, cur_ref, next_ref, tile, nt)
    cw = cw_ref[...]
    xc = cb_ref[...] + sum(_shift_rows(e, 2 - j) * cw[j:j + 1] for j in range(4))
    xc = xc[HALO:HALO + tm]
    xb = xc.astype(BF16)
    r = jax.nn.sigmoid(jnp.dot(xb, wa_ref[...], preferred_element_type=F32) + br_ref[...])
    gate_i = jax.nn.sigmoid(jnp.dot(xb, wx_ref[...], preferred_element_type=F32) + bi_ref[...])
    nl = -lam_ref[...]
    softplus = jnp.maximum(nl, 0.0) + jnp.log1p(jnp.exp(-jnp.abs(nl)))
    a = jnp.exp(-LRU_C * r * softplus)
    bb = jnp.sqrt(1.0 - a * a) * (gate_i * xc)

    row = lax.broadcasted_iota(jnp.int32, (tm, 1), 0) % 8
    for k in (1, 2, 4):
        sh = -k if reverse else k
        take = (row < 8 - k) if reverse else (row >= k)
        bb = jnp.where(take, a * _shift_rows(bb, sh) + bb, bb)
        a = jnp.where(take, a * _shift_rows(a, sh), a)
    a_s[...] = a
    b_s[...] = bb
    ngroups = tm // 8

    def body(g, hprev):
        gg = ngroups - 1 - g if reverse else g
        sl = pl.ds(pl.multiple_of(gg * 8, 8), 8)
        rows = b_s[sl, :] + a_s[sl, :] * hprev
        o_ref[sl, :] = rows
        edge = rows[0:1] if reverse else rows[7:8]
        return jnp.broadcast_to(edge, (8, ch))

    carry[...] = lax.fori_loop(0, ngroups, body, carry[...], unroll=8)


def _lru_call(p, h0, cw, cb, wa, br, wx, bi, lam, reverse):
    b, l, _ = p.shape
    ch = wa.shape[0]
    tm = min(l, 1024)
    nt = l // tm
    tile_of = (lambda i: nt - 1 - i) if reverse else (lambda i: i)
    vec = pl.BlockSpec((1, ch), lambda bi_, i: (0, 0))
    mat = pl.BlockSpec((ch, ch), lambda bi_, i: (0, 0))
    return pl.pallas_call(
        functools.partial(_lru_kernel, reverse=reverse),
        grid=(b, nt),
        in_specs=_halo_specs(tm, ch, 3, nt, tile_of) + [
            pl.BlockSpec((4, ch), lambda bi_, i: (0, 0)), vec, mat, vec, mat, vec, vec,
            pl.BlockSpec((None, 1, ch), lambda bi_, i: (bi_, 0, 0))],
        out_specs=pl.BlockSpec((None, tm, ch), lambda bi_, i: (bi_, tile_of(i), 0)),
        out_shape=jax.ShapeDtypeStruct((b, l, ch), F32),
        scratch_shapes=[pltpu.VMEM((tm, ch), F32), pltpu.VMEM((tm, ch), F32), pltpu.VMEM((8, ch), F32)],
        compiler_params=_params("parallel", "arbitrary"),
        name="rglru_bwd" if reverse else "rglru_fwd",
    )(p, p, p, cw, cb.reshape(1, ch), wa, br.reshape(1, ch), wx, bi.reshape(1, ch), lam.reshape(1, ch), h0)


def _hy_conv_kernel(prev_ref, cur_ref, next_ref, w_ref, b_ref, o_ref):
    i = pl.program_id(2)
    tm = cur_ref.shape[0]
    e = _with_halo(prev_ref, cur_ref, next_ref, i, pl.num_programs(2))
    w = w_ref[...]
    z = b_ref[...] + sum(_shift_rows(e, 1 - j) * w[j:j + 1] for j in range(3))
    o_ref[...] = z[HALO:HALO + tm]


def _hy_conv_call(p, w, bias):
    b, l, _ = p.shape
    ch = 256
    tm = min(l, 2048)
    nt = l // tm
    specs = _halo_specs(tm, ch, 0, nt, nlead=3)
    def with_col(spec):
        inner = spec.index_map
        return pl.BlockSpec(spec.block_shape, lambda bi, j, i: inner(bi, j, i)[:2] + (5 + j,))
    return pl.pallas_call(
        _hy_conv_kernel,
        grid=(b, 3, nt),
        in_specs=[with_col(s) for s in specs] + [
            pl.BlockSpec((3, ch), lambda bi, j, i: (0, j)),
            pl.BlockSpec((1, ch), lambda bi, j, i: (0, j))],
        out_specs=pl.BlockSpec((None, None, tm, ch), lambda bi, j, i: (j, bi, i, 0)),
        out_shape=jax.ShapeDtypeStruct((3, b, l, ch), F32),
        compiler_params=_params("parallel", "parallel", "parallel"),
        name="hyena_short_conv",
    )(p, p, p, w, bias.reshape(1, -1))


def _dot_bf16x3(a, b):
    a_hi = a.astype(BF16)
    b_hi = b.astype(BF16)
    a_lo = (a - a_hi.astype(F32)).astype(BF16)
    b_lo = (b - b_hi.astype(F32)).astype(BF16)
    dot = functools.partial(jnp.dot, preferred_element_type=F32)
    return dot(a_hi, b_hi) + dot(a_lo, b_hi) + dot(a_hi, b_lo)


def _hy_filter_kernel(emb_ref, w1_ref, b1_ref, fq_ref, w2_ref, b2_ref, w3_ref, dl_ref, f_ref, nrm_ref, *, seq_len):
    i = pl.program_id(0)
    tr = emb_ref.shape[0]
    emb = emb_ref[...]
    fq = fq_ref[...]
    h = jnp.sin(fq * (jnp.dot(emb, w1_ref[...], precision=HIGHEST, preferred_element_type=F32) + b1_ref[...]))
    h = jnp.sin(fq * (jnp.dot(h, w2_ref[...], precision=HIGHEST, preferred_element_type=F32) + b2_ref[...]))
    h = _dot_bf16x3(h, w3_ref[...])
    n = i * tr + lax.broadcasted_iota(jnp.int32, (tr, 1), 0)
    filt = jnp.where(n == seq_len, 0.0, h) * jnp.exp(-emb[:, 0:1] * dl_ref[...])
    f_ref[...] = filt

    @pl.when(i == 0)
    def _():
        nrm_ref[...] = jnp.zeros(nrm_ref.shape, F32)

    nrm_ref[...] += jnp.sum(jnp.abs(filt), axis=0, keepdims=True)


def _hy_filter_call(emb_circ, w1, b1, fq, w2, b2, w3, deltas2, seq_len):
    n2l = emb_circ.shape[0]
    hid = w2.shape[0]
    cols = w3.shape[1] // 2
    tr = min(seq_len, 512)
    full = lambda shape: pl.BlockSpec(shape, lambda i: (0, 0))
    return pl.pallas_call(
        functools.partial(_hy_filter_kernel, seq_len=seq_len),
        grid=(n2l // tr,),
        in_specs=[pl.BlockSpec((tr, HY_EMB_PAD), lambda i: (i, 0)),
                  full((HY_EMB_PAD, hid)), full((1, hid)), full((1, hid)),
                  full((hid, hid)), full((1, hid)),
                  pl.BlockSpec((hid, cols), lambda i: (0, (i * tr) // seq_len)),
                  full((1, cols))],
        out_specs=[pl.BlockSpec((tr, cols), lambda i: (i, 0)), full((1, cols))],
        out_shape=[jax.ShapeDtypeStruct((n2l, cols), F32), jax.ShapeDtypeStruct((1, cols), F32)],
        compiler_params=_params("arbitrary"),
        name="hyena_filter",
    )(emb_circ, w1, b1.reshape(1, hid), fq.reshape(1, hid), w2, b2.reshape(1, hid), w3, deltas2)


def _outer_matmuls(fmat, x):
    xt = jnp.swapaxes(x, 0, 1).astype(BF16)
    ys = [jnp.dot(fmat, xt[s], preferred_element_type=F32) for s in range(x.shape[1])]
    return jnp.swapaxes(jnp.stack(ys, axis=0), 0, 1)


def _dft_outer_kernel(f_ref, x_ref, o_ref):
    m = o_ref.shape[1]
    y = _outer_matmuls(f_ref[...], x_ref[...]).astype(BF16)
    o_ref[0] = y[:m]
    o_ref[1] = y[m:]


def _dft_outer_call(fmat, x, lead):
    _, g, k, n2, w = x.shape
    m = fmat.shape[0] // 2
    return pl.pallas_call(
        _dft_outer_kernel,
        grid=(g, n2 // SLAB, w // 256),
        in_specs=[pl.BlockSpec((2 * m, k), lambda gi, j, c: (0, 0)),
                  pl.BlockSpec((None, None, k, SLAB, 256), lambda gi, j, c: (lead, gi, 0, j, c))],
        out_specs=pl.BlockSpec((None, 2, m, SLAB, 256), lambda gi, j, c: (gi, 0, 0, j, c)),
        out_shape=jax.ShapeDtypeStruct((g, 2, m, n2, w), BF16),
        compiler_params=_params("parallel", "parallel", "parallel"),
        name="dft_outer",
    )(fmat, x)


def _cmul(ar, ai, br, bi):
    return ar * br - ai * bi, ar * bi + ai * br


def _twiddled_dft_block(f_ref, tw_ref, k):
    gr, gi = _cmul(f_ref[0], f_ref[1], tw_ref[0, k:k + 1, :], tw_ref[1, k:k + 1, :])
    return jnp.concatenate([jnp.concatenate([gr, -gi], axis=1),
                            jnp.concatenate([gi, gr], axis=1)], axis=0).astype(BF16)


def _hy_spec_kernel(a_ref, tw_ref, f_ref, nrm_ref, h_ref):
    n2 = a_ref.shape[2]
    inv = 1.0 / nrm_ref[...]
    for k in range(a_ref.shape[1]):
        a = jnp.concatenate([a_ref[0, k], a_ref[1, k]], axis=0)
        z = jnp.dot(_twiddled_dft_block(f_ref, tw_ref, k), a, preferred_element_type=F32)
        h_ref[k, 0] = z[:n2] * inv
        h_ref[k, 1] = z[n2:] * inv


def _hy_spec_call(a, tw, fri, nrm):
    _, _, n1, n2, cols = a.shape
    kb = min(n1, OUTER_FREQS_PER_STEP)
    return pl.pallas_call(
        _hy_spec_kernel,
        grid=(n1 // kb,),
        in_specs=[pl.BlockSpec((None, 2, kb, n2, cols), lambda k: (0, 0, k, 0, 0)),
                  pl.BlockSpec((2, kb, n2), lambda k: (0, k, 0)),
                  pl.BlockSpec((2, n2, n2), lambda k: (0, 0, 0)),
                  pl.BlockSpec((1, cols), lambda k: (0, 0))],
        out_specs=pl.BlockSpec((kb, 2, n2, cols), lambda k: (k, 0, 0, 0)),
        out_shape=jax.ShapeDtypeStruct((n1, 2, n2, cols), F32),
        compiler_params=_params("parallel"),
        name="hyena_filter_spectrum",
    )(a, tw, fri, nrm)


def _hy_mid_kernel(a_ref, tw_ref, f_ref, h_ref, o_ref):
    n2 = a_ref.shape[2]
    for k in range(a_ref.shape[1]):
        g = _twiddled_dft_block(f_ref, tw_ref, k)
        a = jnp.concatenate([a_ref[0, k], a_ref[1, k]], axis=0)
        z = jnp.dot(g, a, preferred_element_type=F32)
        yr, yi = _cmul(z[:n2], z[n2:], h_ref[k, 0], h_ref[k, 1])
        y = jnp.concatenate([yr, yi], axis=0).astype(BF16)
        bk = lax.dot_general(g, y, (((0,), (0,)), ((), ())), preferred_element_type=F32)
        o_ref[0, k] = bk[:n2].astype(BF16)
        o_ref[1, k] = bk[n2:].astype(BF16)


def _hy_mid_call(a, tw, fri, spec, order):
    b, _, n1, n2, ch = a.shape
    kb = min(n1, OUTER_FREQS_PER_STEP)
    return pl.pallas_call(
        _hy_mid_kernel,
        grid=(b, n1 // kb),
        in_specs=[pl.BlockSpec((None, 2, kb, n2, ch), lambda bi, k: (bi, 0, k, 0, 0)),
                  pl.BlockSpec((2, kb, n2), lambda bi, k: (0, k, 0)),
                  pl.BlockSpec((2, n2, n2), lambda bi, k: (0, 0, 0)),
                  pl.BlockSpec((kb, 2, n2, ch), lambda bi, k: (k, 0, 0, order))],
        out_specs=pl.BlockSpec((None, 2, kb, n2, ch), lambda bi, k: (bi, 0, k, 0, 0)),
        out_shape=jax.ShapeDtypeStruct(a.shape, BF16),
        compiler_params=_params("parallel", "parallel"),
        name="hyena_spectral_mid",
    )(a, tw, fri, spec)


def _hy_out_kernel(f_ref, b_ref, u_ref, g_ref, sk_ref, o_ref):
    _, n1, s, ch = b_ref.shape
    y = _outer_matmuls(f_ref[...], b_ref[...].astype(F32).reshape(2 * n1, s, ch))
    o_ref[...] = g_ref[...] * (y + sk_ref[...] * u_ref[...])


def _hy_out_call(finv, bm, u, lead, z, order, skip):
    b, _, n1, n2, ch = bm.shape
    m = finv.shape[0]
    return pl.pallas_call(
        _hy_out_kernel,
        grid=(b, n2 // SLAB),
        in_specs=[pl.BlockSpec((m, 2 * n1), lambda bi, j: (0, 0)),
                  pl.BlockSpec((None, 2, n1, SLAB, ch), lambda bi, j: (bi, 0, 0, j, 0)),
                  pl.BlockSpec((None, None, m, SLAB, ch), lambda bi, j: (lead, bi, 0, j, 0)),
                  pl.BlockSpec((None, None, m, SLAB, ch), lambda bi, j: (order, bi, 0, j, 0)),
                  pl.BlockSpec((None, 1, ch), lambda bi, j: (order, 0, 0))],
        out_specs=pl.BlockSpec((None, m, SLAB, ch), lambda bi, j: (bi, 0, j, 0)),
        out_shape=jax.ShapeDtypeStruct((b, m, n2, ch), F32),
        compiler_params=_params("parallel", "parallel"),
        name="hyena_dft_out",
    )(finv, bm, u, z, skip)


def _hy_small_kernel(z_ref, filt_ref, nrm_ref, fs_ref, finv_ref, sk_ref, o_ref):
    l = z_ref.shape[1]
    ch = z_ref.shape[2]
    fs = fs_ref[...]
    n = fs.shape[0] // 2
    spec = jnp.dot(fs, filt_ref[...], precision=HIGHEST, preferred_element_type=F32) / nrm_ref[...]
    sig = z_ref[2]
    for order in range(2):
        u = jnp.dot(fs[:, :l], sig, precision=HIGHEST, preferred_element_type=F32)
        hr = spec[:n, order * ch:(order + 1) * ch]
        hi = spec[n:, order * ch:(order + 1) * ch]
        yr, yi = _cmul(u[:n], u[n:], hr, hi)
        y = jnp.dot(finv_ref[...], jnp.concatenate([yr, yi], axis=0), precision=HIGHEST, preferred_element_type=F32)
        sig = z_ref[order] * (y + sk_ref[order:order + 1] * sig)
    o_ref[...] = sig


def _hy_small_call(z, filt, nrm, fs, finv, skip):
    _, b, l, ch = z.shape
    full = lambda a: pl.BlockSpec(a.shape, lambda bi: (0,) * a.ndim)
    return pl.pallas_call(
        _hy_small_kernel,
        grid=(b,),
        in_specs=[pl.BlockSpec((3, None, l, ch), lambda bi: (0, bi, 0, 0)),
                  full(filt), full(nrm), full(fs), full(finv), full(skip)],
        out_specs=pl.BlockSpec((None, l, ch), lambda bi: (bi, 0, 0)),
        out_shape=jax.ShapeDtypeStruct((b, l, ch), F32),
        compiler_params=_params("parallel"),
        name="hyena_short_sequence",
    )(z, filt, nrm, fs, finv, skip)


def _post_kernel(x_ref, ya_ref, yb_ref, hf_ref, hb_ref, gt_ref, yd_ref, mod_ref, wo_ref, g1_ref, g2_ref, g3_ref,
                 w1_ref, w2_ref, o_ref):
    m = mod_ref[...]
    gt = gt_ref[...]
    gelu = 0.5 * gt * (1.0 + jnp.tanh(math.sqrt(2.0 / math.pi) * (gt + 0.044715 * gt * gt * gt)))
    yc = (hf_ref[...] + hb_ref[...]) * gelu
    cat = jnp.concatenate([ya_ref[...].astype(BF16), yb_ref[...], yc.astype(BF16), yd_ref[...].astype(BF16)], axis=-1)
    mix = jnp.dot(cat, wo_ref[...], preferred_element_type=F32)
    x1 = x_ref[...] + m[2:3] * _rms(mix, g1_ref[...])
    h = (_rms(x1, g2_ref[...]) * (1.0 + m[4:5]) + m[3:4]).astype(BF16)
    dff = w1_ref.shape[1]
    fc = 1024
    acc = jnp.zeros(x1.shape, F32)
    for c in range(dff // fc):
        a = jnp.maximum(jnp.dot(h, w1_ref[:, c * fc:(c + 1) * fc], preferred_element_type=F32), 0.0)
        acc = acc + jnp.dot((a * a).astype(BF16), w2_ref[c * fc:(c + 1) * fc, :], preferred_element_type=F32)
    o_ref[...] = x1 + m[5:6] * _rms(acc, g3_ref[...])


def _post_call(x, ya, yb, hf, hb, p, yd, mod, wo, g1, g2, g3, w1, w2):
    b, l, d = x.shape
    ch = ya.shape[-1]
    tm = min(l, 512)
    row = lambda w, col=0: pl.BlockSpec((None, tm, w), lambda bi, i: (bi, i, col))
    vec = pl.BlockSpec((1, d), lambda bi, i: (0, 0))
    return pl.pallas_call(
        _post_kernel,
        grid=(b, l // tm),
        in_specs=[row(d), row(ch), row(ch), row(ch), row(ch), row(ch, 4), row(ch),
                  pl.BlockSpec((None, 6, d), lambda bi, i: (bi, 0, 0)),
                  _resident(wo.shape, lambda bi, i: (0, 0)), vec, vec, vec,
                  _resident(w1.shape, lambda bi, i: (0, 0)),
                  _resident(w2.shape, lambda bi, i: (0, 0))],
        out_specs=row(d),
        out_shape=jax.ShapeDtypeStruct((b, l, d), F32),
        compiler_params=_params("parallel", "parallel"),
        name="out_proj_mlp",
    )(x, ya, yb, hf, hb, p, yd, mod, wo, g1.reshape(1, d), g2.reshape(1, d), g3.reshape(1, d), w1, w2)


def _rope_tables(seq_len):
    t = jnp.arange(seq_len)
    inv = ROPE_THETA ** (-jnp.arange(ROPE_FREQS, dtype=F32) / ROPE_FREQS)
    ang_r = (t // GRID_W).astype(F32)[:, None] * inv
    ang_c = (t % GRID_W).astype(F32)[:, None] * inv
    cos = jnp.concatenate([jnp.cos(ang_r)] * 2 + [jnp.cos(ang_c)] * 2, axis=1)
    sin = jnp.concatenate([-jnp.sin(ang_r), jnp.sin(ang_r), -jnp.sin(ang_c), jnp.sin(ang_c)], axis=1)
    return jnp.tile(cos, (1, 2)), jnp.tile(sin, (1, 2))


def _hyena_embedding(seq_len):
    def circular(v):
        return jnp.concatenate([v, v[:1], v[:0:-1]])

    t01 = circular(jnp.linspace(0.0, 1.0, seq_len, dtype=F32))[:, None]
    w = circular(2.0 * math.pi * jnp.arange(seq_len, dtype=F32) / seq_len)
    bands = jnp.linspace(1e-4, HY_BANDS - 1, HY_BANDS, dtype=F32)
    z = w[:, None] * bands[None, :]
    pad = jnp.zeros((2 * seq_len, HY_EMB_PAD - HY_EMB), F32)
    return jnp.concatenate([t01, jnp.cos(z), -jnp.sin(z), pad], axis=-1)


def _dft_cos_sin(rows, cols, n):
    ang = (2.0 * math.pi / n) * ((jnp.arange(rows)[:, None] * jnp.arange(cols)[None, :]) % n).astype(F32)
    return jnp.cos(ang), -jnp.sin(ang)


def _dft_tables(seq_len):
    n = 2 * seq_len
    n2 = DFT_N2
    n1 = n // n2
    fr, fi = _dft_cos_sin(n1, n1, n1)
    f_outer = jnp.concatenate([fr, fi], axis=0).astype(BF16)
    f_inv = (jnp.concatenate([fr, fi], axis=1)[:n1 // 2] * (1.0 / n)).astype(BF16)
    f_inner = jnp.stack(_dft_cos_sin(n2, n2, n2))
    twiddle = jnp.stack(_dft_cos_sin(n1, n2, n))
    return f_outer, f_inv, f_inner, twiddle


def _dense_dft_tables(seq_len):
    n = 2 * seq_len
    fr, fi = _dft_cos_sin(n, n, n)
    fs = jnp.concatenate([fr, fi], axis=0)
    finv = jnp.concatenate([fr, fi], axis=1)[:seq_len] * (1.0 / n)
    return fs, finv


def _block_diag(w):
    g, c, _ = w.shape
    eye = jnp.eye(g, dtype=w.dtype)
    return (eye[:, None, :, None] * w[:, :, None, :]).reshape(g * c, g * c)


def _hyena_long(p, l_w, tables, filt, nrm):
    b, l, _ = p.shape
    f_outer, f_inv, f_inner, twiddle = tables
    n2 = DFT_N2
    n1 = 2 * l // n2
    ch = 256
    z = _hy_conv_call(p, l_w["hy_conv_w"], l_w["hy_conv_b"]).reshape(3, b, n1 // 2, n2, ch)
    fa = _dft_outer_call(f_outer, filt.reshape(1, 1, n1, n2, 2 * ch), 0)
    spec = _hy_spec_call(fa, twiddle, f_inner, nrm)
    skip = l_w["hy_skip"].reshape(2, 1, ch)
    sig, lead = z, 2
    for order in range(2):
        a = _dft_outer_call(f_outer[:, :n1 // 2], sig, lead)
        bm = _hy_mid_call(a, twiddle, f_inner, spec, order)
        sig, lead = _hy_out_call(f_inv, bm, sig, lead, z, order, skip)[None], 0
    return sig.reshape(b, l, ch)


def _layer_weights(args, l):
    return {k: v[l] for k, v in args.items()}


def kernel(x, c, ctx, c_ctx, w_mod, b_mod, g_pre_mix, g_post_mix, g_pre_mlp, g_post_mlp, w_in, w_out, pool_w, pool_scale, q_norm_g, k_norm_g, lru_conv_w, lru_conv_b, lru_wa, lru_ba, lru_wx, lru_bx, lru_lambda, hy_conv_w, hy_conv_b, hy_w1, hy_b1, hy_freq, hy_w2, hy_b2, hy_w3, hy_skip, mlp_w1, mlp_w2):
    b, l, d = x.shape
    lc = ctx.shape[1]
    depth = w_mod.shape[0]
    per_layer = dict(g_pre_mix=g_pre_mix, g_post_mix=g_post_mix, g_pre_mlp=g_pre_mlp, g_post_mlp=g_post_mlp,
                     w_in=w_in.astype(BF16), w_out=w_out.astype(BF16), pool_w=pool_w, pool_scale=pool_scale,
                     q_norm_g=q_norm_g, k_norm_g=k_norm_g, lru_conv_w=lru_conv_w, lru_conv_b=lru_conv_b,
                     lru_wa=lru_wa, lru_ba=lru_ba, lru_wx=lru_wx, lru_bx=lru_bx, lru_lambda=lru_lambda,
                     hy_conv_w=hy_conv_w, hy_conv_b=hy_conv_b, hy_w1=hy_w1, hy_b1=hy_b1, hy_freq=hy_freq,
                     hy_w2=hy_w2, hy_b2=hy_b2, hy_w3=hy_w3, hy_skip=hy_skip,
                     mlp_w1=mlp_w1.astype(BF16), mlp_w2=mlp_w2.astype(BF16))

    cos, sin = _rope_tables(l)
    avg = _block_diag(jnp.full((2, HEAD_DIM, HEAD_DIM), 1.0 / HEAD_DIM, F32)).astype(BF16)
    emb_x, emb_c = _hyena_embedding(l), _hyena_embedding(lc)
    dft_x = _dft_tables(l)
    fs_c, finv_c = _dense_dft_tables(lc)
    deltas = jnp.abs(jnp.linspace(HY_MIN_DECAY, HY_MAX_DECAY, 256, dtype=F32))
    deltas2 = jnp.tile(deltas, 2).reshape(1, 512)

    cond = jnp.zeros((8, d), F32).at[:b].set(c).at[b].set(c_ctx)
    mods = _mod_call(cond, w_mod, b_mod).reshape(depth, 8, 6, d)

    cx = ctx
    for li in range(depth):
        w = _layer_weights(per_layer, li)
        need_ctx = li < depth - 1
        mod_x = mods[li, :b]
        mod_c = jnp.broadcast_to(mods[li, b:b + 1], (b, 6, d))
        w1p = jnp.pad(w["hy_w1"], ((0, HY_EMB_PAD - HY_EMB), (0, 0)))
        pool_bd = _block_diag(w["pool_w"]).astype(BF16)
        gq = jnp.tile(w["q_norm_g"], 2).reshape(1, 128)
        gk = jnp.tile(w["k_norm_g"], 2).reshape(1, 128)
        wa = [_block_diag(w["lru_wa"][dr]).astype(BF16) for dr in range(2)]
        wx = [_block_diag(w["lru_wx"][dr]).astype(BF16) for dr in range(2)]

        px = _inproj_call(x, mod_x, w["g_pre_mix"], w["w_in"])
        pc = _inproj_call(cx, mod_c, w["g_pre_mix"], w["w_in"])

        qtx, kx, vtx = _qkv_call(px, cos, sin, gq, gk, avg)
        qtc, kc, vtc = _qkv_call(pc, None, None, gq, gk, avg)
        yb_x = _flash_call(qtx, kx, vtx, kc, vtc)

        hs_c, hs_x = [], []
        for dr in range(2):
            lw = (w["lru_conv_w"], w["lru_conv_b"], wa[dr], w["lru_ba"][dr], wx[dr], w["lru_bx"][dr],
                  w["lru_lambda"][dr])
            hc = _lru_call(pc, jnp.zeros((b, 1, 256), F32), *lw, reverse=bool(dr))
            h0 = hc[:, :1] if dr else hc[:, -1:]
            hs_c.append(hc)
            hs_x.append(_lru_call(px, h0, *lw, reverse=bool(dr)))

        ya_x = _pool_call(px, pool_bd, w["pool_scale"])
        hy = (w1p, w["hy_b1"], w["hy_freq"], w["hy_w2"], w["hy_b2"], w["hy_w3"], deltas2)
        filt_x, nrm_x = _hy_filter_call(emb_x, *hy, seq_len=l)
        yd_x = _hyena_long(px, w, dft_x, filt_x, nrm_x)

        x = _post_call(x, ya_x, yb_x, hs_x[0], hs_x[1], px, yd_x, mod_x, w["w_out"], w["g_post_mix"],
                       w["g_pre_mlp"], w["g_post_mlp"], w["mlp_w1"], w["mlp_w2"])

        if need_ctx:
            yb_c = _flash_call(qtc, kc, vtc)
            ya_c = _pool_call(pc, pool_bd, w["pool_scale"])
            filt_c, nrm_c = _hy_filter_call(emb_c, *hy, seq_len=lc)
            z_c = _hy_conv_call(pc, w["hy_conv_w"], w["hy_conv_b"])
            yd_c = _hy_small_call(z_c, filt_c, nrm_c, fs_c, finv_c, w["hy_skip"])
            cx = _post_call(cx, ya_c, yb_c, hs_c[0], hs_c[1], pc, yd_c, mod_c, w["w_out"], w["g_post_mix"],
                            w["g_pre_mlp"], w["g_post_mlp"], w["mlp_w1"], w["mlp_w2"])
    return x
```

```python
import functools
import math

import jax
import jax.numpy as jnp
from jax import lax
from jax.experimental import pallas as pl
from jax.experimental.pallas import tpu as pltpu

F32 = jnp.float32
BF16 = jnp.bfloat16
EPS = 1e-6

GRID_W = 64
HEAD_DIM = 64
QUERY_SCALE = HEAD_DIM ** -0.5 * math.log2(math.e)
N_KV_HEADS = 2
ROPE_FREQS = 16
ROPE_THETA = 10000.0
POOL_WINDOWS = (2, 4, 8, 16)
LRU_C = 8.0
HY_EMB = 33
HY_EMB_PAD = 64
HY_BANDS = 16
HY_MIN_DECAY = math.log(1e-2) / 1.5
HY_MAX_DECAY = math.log(1e-2) / 0.3
DFT_N2 = 128
HALO = 16
SLAB = 16
OUTER_FREQS_PER_STEP = 8
UNIT_KEYS = 4096
QUERY_CHAIN = 512
VMEM_LIMIT = 56 * 1024 * 1024
HIGHEST = lax.Precision.HIGHEST


def _params(*sem):
    return pltpu.CompilerParams(dimension_semantics=sem, vmem_limit_bytes=VMEM_LIMIT)


def _resident(shape, index_map):
    return pl.BlockSpec(shape, index_map, pipeline_mode=pl.Buffered(1))


def _rms(x, g):
    return x * lax.rsqrt(jnp.mean(x * x, axis=-1, keepdims=True) + EPS) * g


def _shift_rows(x, k):
    n = x.shape[0]
    return pltpu.roll(x, k % n, 0)


def _with_halo(prev_ref, cur_ref, next_ref, tile, ntiles):
    prev = jnp.where(tile > 0, prev_ref[...], 0.0)
    nxt = jnp.where(tile < ntiles - 1, next_ref[...], 0.0)
    return jnp.concatenate([prev, cur_ref[...], nxt], axis=0)


def _halo_specs(tm, width, col, ntiles, tile_of=lambda i: i, nlead=2):
    r = tm // HALO
    last = ntiles * r - 1

    def prev_map(*g):
        return (g[0], jnp.maximum(tile_of(g[nlead - 1]) * r - 1, 0), col)

    def cur_map(*g):
        return (g[0], tile_of(g[nlead - 1]), col)

    def next_map(*g):
        return (g[0], jnp.minimum((tile_of(g[nlead - 1]) + 1) * r, last), col)

    return [pl.BlockSpec((None, HALO, width), prev_map),
            pl.BlockSpec((None, tm, width), cur_map),
            pl.BlockSpec((None, HALO, width), next_map)]


def _mod_kernel(c_ref, w_ref, b_ref, o_ref):
    cond = c_ref[...]
    act = (cond * jax.nn.sigmoid(cond)).astype(BF16)
    o_ref[...] = jnp.dot(act, w_ref[...].astype(BF16), preferred_element_type=F32) + b_ref[...]


def _mod_call(cond, w_mod, b_mod):
    depth, d, n = w_mod.shape
    tn = 1024
    return pl.pallas_call(
        _mod_kernel,
        grid=(depth, n // tn),
        in_specs=[pl.BlockSpec((8, d), lambda l, j: (0, 0)),
                  pl.BlockSpec((None, d, tn), lambda l, j: (l, 0, j)),
                  pl.BlockSpec((None, 1, tn), lambda l, j: (l, 0, j))],
        out_specs=pl.BlockSpec((None, 8, tn), lambda l, j: (l, 0, j)),
        out_shape=jax.ShapeDtypeStruct((depth, 8, n), F32),
        compiler_params=_params("parallel", "parallel"),
        name="adaln_mod",
    )(cond, w_mod, b_mod.reshape(depth, 1, n))


def _inproj_kernel(x_ref, mod_ref, g_ref, w_ref, o_ref):
    m = mod_ref[...]
    h = _rms(x_ref[...], g_ref[...]) * (1.0 + m[1:2]) + m[0:1]
    o_ref[...] = jnp.dot(h.astype(BF16), w_ref[...], preferred_element_type=F32)


def _inproj_call(x, mod, g, w):
    b, l, d = x.shape
    n = w.shape[1]
    tm = min(l, 1024)
    return pl.pallas_call(
        _inproj_kernel,
        grid=(b, l // tm),
        in_specs=[pl.BlockSpec((None, tm, d), lambda bi, i: (bi, i, 0)),
                  pl.BlockSpec((None, 6, d), lambda bi, i: (bi, 0, 0)),
                  pl.BlockSpec((1, d), lambda bi, i: (0, 0)),
                  _resident((d, n), lambda bi, i: (0, 0))],
        out_specs=pl.BlockSpec((None, tm, n), lambda bi, i: (bi, i, 0)),
        out_shape=jax.ShapeDtypeStruct((b, l, n), F32),
        compiler_params=_params("parallel", "parallel"),
        name="in_proj",
    )(x, mod, g.reshape(1, d), w)


def _pool_kernel(prev_ref, cur_ref, next_ref, w_ref, sc_ref, o_ref, *, seq_len):
    i = pl.program_id(1)
    tm, ch = cur_ref.shape
    e = _with_halo(prev_ref, cur_ref, next_ref, i, pl.num_programs(1))
    s2 = e + _shift_rows(e, 1)
    s4 = _shift_rows(s2, 1) + _shift_rows(s2, -1)
    s8 = _shift_rows(s4, 2) + _shift_rows(s4, -2)
    s16 = _shift_rows(s8, 4) + _shift_rows(s8, -4)
    t = i * tm + lax.broadcasted_iota(jnp.int32, (tm, 1), 0)
    group = lax.broadcasted_iota(jnp.int32, (1, ch), 1) // (ch // len(POOL_WINDOWS))
    mean = jnp.zeros((tm, ch), F32)
    for gi, (win, s) in enumerate(zip(POOL_WINDOWS, (s2, s4, s8, s16))):
        lo = jnp.maximum(t - win // 2, 0)
        hi = jnp.minimum(t + win - win // 2, seq_len)
        m = s[HALO:HALO + tm] / (hi - lo).astype(F32)
        mean = jnp.where(group == gi, m, mean)
    d = (mean - cur_ref[...]).astype(BF16)
    o_ref[...] = jnp.dot(d, w_ref[...], preferred_element_type=F32) * sc_ref[...]


def _pool_call(p, w_bd, scale):
    b, l, _ = p.shape
    ch = w_bd.shape[0]
    tm = min(l, 1024)
    nt = l // tm
    return pl.pallas_call(
        functools.partial(_pool_kernel, seq_len=l),
        grid=(b, nt),
        in_specs=_halo_specs(tm, ch, 0, nt) + [
            pl.BlockSpec((ch, ch), lambda bi, i: (0, 0)),
            pl.BlockSpec((1, ch), lambda bi, i: (0, 0))],
        out_specs=pl.BlockSpec((None, tm, ch), lambda bi, i: (bi, i, 0)),
        out_shape=jax.ShapeDtypeStruct((b, l, ch), F32),
        compiler_params=_params("parallel", "parallel"),
        name="pool_mixer",
    )(p, p, p, w_bd, scale.reshape(1, ch))


def _head_norm(x, g, avg):
    sq = x * x
    hi = sq.astype(BF16)
    lo = (sq - hi.astype(F32)).astype(BF16)
    ms = (jnp.dot(hi, avg, preferred_element_type=F32) + jnp.dot(lo, avg, preferred_element_type=F32))
    return x * lax.rsqrt(ms + EPS) * g


def _rope(x, cos, sin):
    n = x.shape[1]
    lane = lax.broadcasted_iota(jnp.int32, (1, n), 1)
    partner = jnp.where(lane % 32 < 16, pltpu.roll(x, n - 16, 1), pltpu.roll(x, 16, 1))
    return x * cos + partner * sin


def _qkv_kernel(*refs, rope):
    if rope:
        q_ref, kv_ref, cos_ref, sin_ref, gq_ref, gk_ref, avg_ref, qt_ref, ko_ref, vt_ref = refs
    else:
        q_ref, kv_ref, gq_ref, gk_ref, avg_ref, qt_ref, ko_ref, vt_ref = refs
    avg = avg_ref[...]
    for half in range(2):
        q = _head_norm(q_ref[:, half * 128:(half + 1) * 128], gq_ref[...], avg)
        if rope:
            q = _rope(q, cos_ref[...], sin_ref[...])
        qt = (q * QUERY_SCALE).T
        for h in range(2):
            qt_ref[2 * half + h] = qt[h * HEAD_DIM:(h + 1) * HEAD_DIM].astype(BF16)
    k = _head_norm(kv_ref[:, 0:128], gk_ref[...], avg)
    if rope:
        k = _rope(k, cos_ref[...], sin_ref[...])
    ko_ref[...] = k.astype(BF16)
    vt = kv_ref[:, 128:256].T
    for h in range(N_KV_HEADS):
        vt_ref[h] = vt[h * HEAD_DIM:(h + 1) * HEAD_DIM].astype(BF16)


def _qkv_call(p, cos, sin, gq, gk, avg):
    b, l, _ = p.shape
    tm = min(l, 512)
    rope = cos is not None
    tables = [cos, sin] if rope else []
    table_specs = [pl.BlockSpec((tm, 128), lambda bi, i: (i, 0))] * 2 if rope else []
    const = pl.BlockSpec((1, 128), lambda bi, i: (0, 0))
    return pl.pallas_call(
        functools.partial(_qkv_kernel, rope=rope),
        grid=(b, l // tm),
        in_specs=[pl.BlockSpec((None, tm, 256), lambda bi, i: (bi, i, 1)),
                  pl.BlockSpec((None, tm, 256), lambda bi, i: (bi, i, 2))] + table_specs + [
                  const, const, pl.BlockSpec((128, 128), lambda bi, i: (0, 0))],
        out_specs=[pl.BlockSpec((None, 2 * N_KV_HEADS, HEAD_DIM, tm), lambda bi, i: (bi, 0, 0, i)),
                   pl.BlockSpec((None, tm, 128), lambda bi, i: (bi, i, 0)),
                   pl.BlockSpec((None, N_KV_HEADS, None, HEAD_DIM, tm), lambda bi, i: (bi, 0, i, 0, 0))],
        out_shape=[jax.ShapeDtypeStruct((b, 2 * N_KV_HEADS, HEAD_DIM, l), BF16),
                   jax.ShapeDtypeStruct((b, l, 128), BF16),
                   jax.ShapeDtypeStruct((b, N_KV_HEADS, l // tm, HEAD_DIM, tm), BF16)],
        compiler_params=_params("parallel", "parallel"),
        name="qkv_prep",
    )(p, p, *tables, gq, gk, avg)


def _flash_kernel(*refs, with_ctx):
    if with_ctx:
        q_ref, k_ref, vt_ref, kc_ref, vtc_ref, o_ref, qs, m_s, l_s, acc, sb0, sb1, sb2 = refs
    else:
        q_ref, k_ref, vt_ref, o_ref, qs, m_s, l_s, acc, sb0, sb1, sb2 = refs
    h = pl.program_id(1)
    kj = pl.program_id(3)
    tq = q_ref.shape[-1]
    qc = qs.shape[-1]
    per_head = tq // qc
    nchain = 2 * per_head
    nkc, _, tkc = vt_ref.shape
    cpu = min(nkc, UNIT_KEYS // tkc)
    ukeys = cpu * tkc
    nunits = nchain * (nkc // cpu)

    def sublane_all(op, x):
        for sh in (1, 2, 4):
            x = op(x, pltpu.roll(x, sh, 0))
        return x

    def softmax_pv(s, vts, c):
        s3 = s.reshape(s.shape[0] // 8, 8, qc)
        m_prev = m_s[c]
        m_new = jnp.maximum(m_prev, sublane_all(jnp.maximum, jnp.max(s3, axis=0)))
        alpha = jnp.exp2(m_prev - m_new)
        p3 = jnp.exp2(s3 - m_new[None])
        l_s[c] = alpha * l_s[c] + jnp.sum(p3, axis=0)
        p = p3.reshape(s.shape).astype(BF16)
        step = s.shape[0] // len(vts)
        pv = sum(jnp.dot(vt, p[i * step:(i + 1) * step], preferred_element_type=F32) for i, vt in enumerate(vts))
        acc[c] = (acc[c].reshape(HEAD_DIM // 8, 8, qc) * alpha[None]).reshape(HEAD_DIM, qc) + pv
        m_s[c] = m_new

    def scores(u, buf):
        rows = pl.ds(pl.multiple_of((u // nchain) * ukeys, ukeys), ukeys)
        buf[...] = jnp.dot(k_ref[rows, :], qs[u % nchain], preferred_element_type=F32)

    def consume(u, buf):
        softmax_pv(buf[...], [vt_ref[(u // nchain) * cpu + i] for i in range(cpu)], u % nchain)

    @pl.when(kj == 0)
    def _():
        qs[...] = jnp.zeros(qs.shape, BF16)
        rows = pl.ds(pl.multiple_of(h * HEAD_DIM, HEAD_DIM), HEAD_DIM)
        for c in range(nchain):
            qs[c, rows, :] = q_ref[c // per_head, :, (c % per_head) * qc:(c % per_head + 1) * qc]
        m_s[...] = jnp.full(m_s.shape, -1e30, F32)
        l_s[...] = jnp.zeros(l_s.shape, F32)
        acc[...] = jnp.zeros(acc.shape, F32)
        if with_ctx:
            for c in range(nchain):
                softmax_pv(jnp.dot(kc_ref[...], qs[c], preferred_element_type=F32), [vtc_ref[...]], c)

    bufs = (sb0, sb1, sb2)
    scores(0, sb0)
    scores(1, sb1)
    trips = (nunits - 2) // 3

    def triple(j, carry):
        for i in range(3):
            scores(3 * j + i + 2, bufs[(i + 2) % 3])
            consume(3 * j + i, bufs[i])
        return carry

    lax.fori_loop(0, trips, triple, 0)
    for u in range(3 * trips, nunits):
        if u + 2 < nunits:
            scores(u + 2, bufs[(u + 2) % 3])
        consume(u, bufs[u % 3])

    @pl.when(kj == pl.num_programs(3) - 1)
    def _():
        for r in range(per_head):
            halves = []
            for c in (r, r + per_head):
                l = sublane_all(jnp.add, l_s[c])
                halves.append((acc[c].reshape(HEAD_DIM // 8, 8, qc) / l[None]).reshape(HEAD_DIM, qc))
            o_ref[r * qc:(r + 1) * qc, :] = jnp.concatenate(halves, axis=0).T.astype(o_ref.dtype)


def _flash_call(qt, k, vt, kc=None, vtc=None):
    b, _, _, lq = qt.shape
    lk = k.shape[1]
    tkc = vt.shape[-1]
    tq = min(lq, 1024)
    tk = lk
    nkc = tk // tkc
    qc = min(tq, QUERY_CHAIN)
    nchain = 2 * tq // qc
    ukeys = min(nkc, UNIT_KEYS // tkc) * tkc
    with_ctx = kc is not None
    in_specs = [pl.BlockSpec((None, 2, HEAD_DIM, tq), lambda bi, h, i, j: (bi, h, 0, i)),
                pl.BlockSpec((None, tk, 128), lambda bi, h, i, j: (bi, j, 0)),
                pl.BlockSpec((None, None, nkc, HEAD_DIM, tkc), lambda bi, h, i, j: (bi, h, j, 0, 0))]
    args = [qt, k, vt]
    if with_ctx:
        c = kc.shape[1]
        in_specs += [pl.BlockSpec((None, c, 128), lambda bi, h, i, j: (bi, 0, 0)),
                     pl.BlockSpec((None, None, None, HEAD_DIM, c), lambda bi, h, i, j: (bi, h, 0, 0, 0))]
        args += [kc, vtc]
    return pl.pallas_call(
        functools.partial(_flash_kernel, with_ctx=with_ctx),
        grid=(b, N_KV_HEADS, lq // tq, lk // tk),
        in_specs=in_specs,
        out_specs=pl.BlockSpec((None, tq, 128), lambda bi, h, i, j: (bi, i, h)),
        out_shape=jax.ShapeDtypeStruct((b, lq, 256), BF16),
        scratch_shapes=[pltpu.VMEM((nchain, 2 * HEAD_DIM, qc), BF16),
                        pltpu.VMEM((nchain, 8, qc), F32),
                        pltpu.VMEM((nchain, 8, qc), F32),
                        pltpu.VMEM((nchain, HEAD_DIM, qc), F32),
                        pltpu.VMEM((ukeys, qc), F32),
                        pltpu.VMEM((ukeys, qc), F32),
                        pltpu.VMEM((ukeys, qc), F32)],
        compiler_params=_params("parallel", "parallel", "parallel", "arbitrary"),
        name="flash_attention",
    )(*args)


def _lru_kernel(prev_ref, cur_ref, next_ref, cw_ref, cb_ref, wa_ref, br_ref, wx_ref, bi_ref, lam_ref, h0_ref,
                o_ref, a_s, b_s, carry, *, reverse):
    i = pl.program_id(1)
    nt = pl.num_programs(1)
    tm, ch = cur_ref.shape
    tile = nt - 1 - i if reverse else i

    @pl.when(i == 0)
    def _():
        carry[...] = jnp.broadcast_to(h0_ref[...], carry.shape)

    e = _with_halo(prev_ref, cur_ref, next_ref, tile, nt)
    cw = cw_ref[...]
    xc = cb_ref[...] + sum(_shift_rows(e, 2 - j) * cw[j:j + 1] for j in range(4))
    xc = xc[HALO:HALO + tm]
    xb = xc.astype(BF16)
    r = jax.nn.sigmoid(jnp.dot(xb, wa_ref[...], preferred_element_type=F32) + br_ref[...])
    gate_i = jax.nn.sigmoid(jnp.dot(xb, wx_ref[...], preferred_element_type=F32) + bi_ref[...])
    nl = -lam_ref[...]
    softplus = jnp.maximum(nl, 0.0) + jnp.log1p(jnp.exp(-jnp.abs(nl)))
    a = jnp.exp(-LRU_C * r * softplus)
    bb = jnp.sqrt(1.0 - a * a) * (gate_i * xc)

    row = lax.broadcasted_iota(jnp.int32, (tm, 1), 0) % 8
    for k in (1, 2, 4):
        sh = -k if reverse else k
        take = (row < 8 - k) if reverse else (row >= k)
        bb = jnp.where(take, a * _shift_rows(bb, sh) + bb, bb)
        a = jnp.where(take, a * _shift_rows(a, sh), a)
    a_s[...] = a
    b_s[...] = bb
    ngroups = tm // 8

    def body(g, hprev):
        gg = ngroups - 1 - g if reverse else g
        sl = pl.ds(pl.multiple_of(gg * 8, 8), 8)
        rows = b_s[sl, :] + a_s[sl, :] * hprev
        o_ref[sl, :] = rows
        edge = rows[0:1] if reverse else rows[7:8]
        return jnp.broadcast_to(edge, (8, ch))

    carry[...] = lax.fori_loop(0, ngroups, body, carry[...], unroll=8)


def _lru_call(p, h0, cw, cb, wa, br, wx, bi, lam, reverse):
    b, l, _ = p.shape
    ch = wa.shape[0]
    tm = min(l, 1024)
    nt = l // tm
    tile_of = (lambda i: nt - 1 - i) if reverse else (lambda i: i)
    vec = pl.BlockSpec((1, ch), lambda bi_, i: (0, 0))
    mat = pl.BlockSpec((ch, ch), lambda bi_, i: (0, 0))
    return pl.pallas_call(
        functools.partial(_lru_kernel, reverse=reverse),
        grid=(b, nt),
        in_specs=_halo_specs(tm, ch, 3, nt, tile_of) + [
            pl.BlockSpec((4, ch), lambda bi_, i: (0, 0)), vec, mat, vec, mat, vec, vec,
            pl.BlockSpec((None, 1, ch), lambda bi_, i: (bi_, 0, 0))],
        out_specs=pl.BlockSpec((None, tm, ch), lambda bi_, i: (bi_, tile_of(i), 0)),
        out_shape=jax.ShapeDtypeStruct((b, l, ch), F32),
        scratch_shapes=[pltpu.VMEM((tm, ch), F32), pltpu.VMEM((tm, ch), F32), pltpu.VMEM((8, ch), F32)],
        compiler_params=_params("parallel", "arbitrary"),
        name="rglru_bwd" if reverse else "rglru_fwd",
    )(p, p, p, cw, cb.reshape(1, ch), wa, br.reshape(1, ch), wx, bi.reshape(1, ch), lam.reshape(1, ch), h0)


def _hy_conv_kernel(prev_ref, cur_ref, next_ref, w_ref, b_ref, o_ref):
    i = pl.program_id(2)
    tm = cur_ref.shape[0]
    e = _with_halo(prev_ref, cur_ref, next_ref, i, pl.num_programs(2))
    w = w_ref[...]
    z = b_ref[...] + sum(_shift_rows(e, 1 - j) * w[j:j + 1] for j in range(3))
    o_ref[...] = z[HALO:HALO + tm]


def _hy_conv_call(p, w, bias):
    b, l, _ = p.shape
    ch = 256
    tm = min(l, 2048)
    nt = l // tm
    specs = _halo_specs(tm, ch, 0, nt, nlead=3)
    def with_col(spec):
        inner = spec.index_map
        return pl.BlockSpec(spec.block_shape, lambda bi, j, i: inner(bi, j, i)[:2] + (5 + j,))
    return pl.pallas_call(
        _hy_conv_kernel,
        grid=(b, 3, nt),
        in_specs=[with_col(s) for s in specs] + [
            pl.BlockSpec((3, ch), lambda bi, j, i: (0, j)),
            pl.BlockSpec((1, ch), lambda bi, j, i: (0, j))],
        out_specs=pl.BlockSpec((None, None, tm, ch), lambda bi, j, i: (j, bi, i, 0)),
        out_shape=jax.ShapeDtypeStruct((3, b, l, ch), F32),
        compiler_params=_params("parallel", "parallel", "parallel"),
        name="hyena_short_conv",
    )(p, p, p, w, bias.reshape(1, -1))


def _dot_bf16x3(a, b):
    a_hi = a.astype(BF16)
    b_hi = b.astype(BF16)
    a_lo = (a - a_hi.astype(F32)).astype(BF16)
    b_lo = (b - b_hi.astype(F32)).astype(BF16)
    dot = functools.partial(jnp.dot, preferred_element_type=F32)
    return dot(a_hi, b_hi) + dot(a_lo, b_hi) + dot(a_hi, b_lo)


def _hy_filter_kernel(emb_ref, w1_ref, b1_ref, fq_ref, w2_ref, b2_ref, w3_ref, dl_ref, f_ref, nrm_ref, *, seq_len):
    i = pl.program_id(0)
    tr = emb_ref.shape[0]
    emb = emb_ref[...]
    fq = fq_ref[...]
    h = jnp.sin(fq * (jnp.dot(emb, w1_ref[...], precision=HIGHEST, preferred_element_type=F32) + b1_ref[...]))
    h = jnp.sin(fq * (jnp.dot(h, w2_ref[...], precision=HIGHEST, preferred_element_type=F32) + b2_ref[...]))
    h = _dot_bf16x3(h, w3_ref[...])
    n = i * tr + lax.broadcasted_iota(jnp.int32, (tr, 1), 0)
    filt = jnp.where(n == seq_len, 0.0, h) * jnp.exp(-emb[:, 0:1] * dl_ref[...])
    f_ref[...] = filt

    @pl.when(i == 0)
    def _():
        nrm_ref[...] = jnp.zeros(nrm_ref.shape, F32)

    nrm_ref[...] += jnp.sum(jnp.abs(filt), axis=0, keepdims=True)


def _hy_filter_call(emb_circ, w1, b1, fq, w2, b2, w3, deltas2, seq_len):
    n2l = emb_circ.shape[0]
    hid = w2.shape[0]
    cols = w3.shape[1] // 2
    tr = min(seq_len, 512)
    full = lambda shape: pl.BlockSpec(shape, lambda i: (0, 0))
    return pl.pallas_call(
        functools.partial(_hy_filter_kernel, seq_len=seq_len),
        grid=(n2l // tr,),
        in_specs=[pl.BlockSpec((tr, HY_EMB_PAD), lambda i: (i, 0)),
                  full((HY_EMB_PAD, hid)), full((1, hid)), full((1, hid)),
                  full((hid, hid)), full((1, hid)),
                  pl.BlockSpec((hid, cols), lambda i: (0, (i * tr) // seq_len)),
                  full((1, cols))],
        out_specs=[pl.BlockSpec((tr, cols), lambda i: (i, 0)), full((1, cols))],
        out_shape=[jax.ShapeDtypeStruct((n2l, cols), F32), jax.ShapeDtypeStruct((1, cols), F32)],
        compiler_params=_params("arbitrary"),
        name="hyena_filter",
    )(emb_circ, w1, b1.reshape(1, hid), fq.reshape(1, hid), w2, b2.reshape(1, hid), w3, deltas2)


def _outer_matmuls(fmat, x):
    xt = jnp.swapaxes(x, 0, 1).astype(BF16)
    ys = [jnp.dot(fmat, xt[s], preferred_element_type=F32) for s in range(x.shape[1])]
    return jnp.swapaxes(jnp.stack(ys, axis=0), 0, 1)


def _dft_outer_kernel(f_ref, x_ref, o_ref):
    m = o_ref.shape[1]
    y = _outer_matmuls(f_ref[...], x_ref[...]).astype(BF16)
    o_ref[0] = y[:m]
    o_ref[1] = y[m:]


def _dft_outer_call(fmat, x, lead):
    _, g, k, n2, w = x.shape
    m = fmat.shape[0] // 2
    return pl.pallas_call(
        _dft_outer_kernel,
        grid=(g, n2 // SLAB, w // 256),
        in_specs=[pl.BlockSpec((2 * m, k), lambda gi, j, c: (0, 0)),
                  pl.BlockSpec((None, None, k, SLAB, 256), lambda gi, j, c: (lead, gi, 0, j, c))],
        out_specs=pl.BlockSpec((None, 2, m, SLAB, 256), lambda gi, j, c: (gi, 0, 0, j, c)),
        out_shape=jax.ShapeDtypeStruct((g, 2, m, n2, w), BF16),
        compiler_params=_params("parallel", "parallel", "parallel"),
        name="dft_outer",
    )(fmat, x)


def _cmul(ar, ai, br, bi):
    return ar * br - ai * bi, ar * bi + ai * br


def _twiddled_dft_block(f_ref, tw_ref, k):
    gr, gi = _cmul(f_ref[0], f_ref[1], tw_ref[0, k:k + 1, :], tw_ref[1, k:k + 1, :])
    return jnp.concatenate([jnp.concatenate([gr, -gi], axis=1),
                            jnp.concatenate([gi, gr], axis=1)], axis=0).astype(BF16)


def _hy_spec_kernel(a_ref, tw_ref, f_ref, nrm_ref, h_ref):
    n2 = a_ref.shape[2]
    inv = 1.0 / nrm_ref[...]
    for k in range(a_ref.shape[1]):
        a = jnp.concatenate([a_ref[0, k], a_ref[1, k]], axis=0)
        z = jnp.dot(_twiddled_dft_block(f_ref, tw_ref, k), a, preferred_element_type=F32)
        h_ref[k, 0] = z[:n2] * inv
        h_ref[k, 1] = z[n2:] * inv


def _hy_spec_call(a, tw, fri, nrm):
    _, _, n1, n2, cols = a.shape
    kb = min(n1, OUTER_FREQS_PER_STEP)
    return pl.pallas_call(
        _hy_spec_kernel,
        grid=(n1 // kb,),
        in_specs=[pl.BlockSpec((None, 2, kb, n2, cols), lambda k: (0, 0, k, 0, 0)),
                  pl.BlockSpec((2, kb, n2), lambda k: (0, k, 0)),
                  pl.BlockSpec((2, n2, n2), lambda k: (0, 0, 0)),
                  pl.BlockSpec((1, cols), lambda k: (0, 0))],
        out_specs=pl.BlockSpec((kb, 2, n2, cols), lambda k: (k, 0, 0, 0)),
        out_shape=jax.ShapeDtypeStruct((n1, 2, n2, cols), F32),
        compiler_params=_params("parallel"),
        name="hyena_filter_spectrum",
    )(a, tw, fri, nrm)


def _hy_mid_kernel(a_ref, tw_ref, f_ref, h_ref, o_ref):
    n2 = a_ref.shape[2]
    for k in range(a_ref.shape[1]):
        g = _twiddled_dft_block(f_ref, tw_ref, k)
        a = jnp.concatenate([a_ref[0, k], a_ref[1, k]], axis=0)
        z = jnp.dot(g, a, preferred_element_type=F32)
        yr, yi = _cmul(z[:n2], z[n2:], h_ref[k, 0], h_ref[k, 1])
        y = jnp.concatenate([yr, yi], axis=0).astype(BF16)
        bk = lax.dot_general(g, y, (((0,), (0,)), ((), ())), preferred_element_type=F32)
        o_ref[0, k] = bk[:n2].astype(BF16)
        o_ref[1, k] = bk[n2:].astype(BF16)


def _hy_mid_call(a, tw, fri, spec, order):
    b, _, n1, n2, ch = a.shape
    kb = min(n1, OUTER_FREQS_PER_STEP)
    return pl.pallas_call(
        _hy_mid_kernel,
        grid=(b, n1 // kb),
        in_specs=[pl.BlockSpec((None, 2, kb, n2, ch), lambda bi, k: (bi, 0, k, 0, 0)),
                  pl.BlockSpec((2, kb, n2), lambda bi, k: (0, k, 0)),
                  pl.BlockSpec((2, n2, n2), lambda bi, k: (0, 0, 0)),
                  pl.BlockSpec((kb, 2, n2, ch), lambda bi, k: (k, 0, 0, order))],
        out_specs=pl.BlockSpec((None, 2, kb, n2, ch), lambda bi, k: (bi, 0, k, 0, 0)),
        out_shape=jax.ShapeDtypeStruct(a.shape, BF16),
        compiler_params=_params("parallel", "parallel"),
        name="hyena_spectral_mid",
    )(a, tw, fri, spec)


def _hy_out_kernel(f_ref, b_ref, u_ref, g_ref, sk_ref, o_ref):
    _, n1, s, ch = b_ref.shape
    y = _outer_matmuls(f_ref[...], b_ref[...].astype(F32).reshape(2 * n1, s, ch))
    o_ref[...] = g_ref[...] * (y + sk_ref[...] * u_ref[...])


def _hy_out_call(finv, bm, u, lead, z, order, skip):
    b, _, n1, n2, ch = bm.shape
    m = finv.shape[0]
    return pl.pallas_call(
        _hy_out_kernel,
        grid=(b, n2 // SLAB),
        in_specs=[pl.BlockSpec((m, 2 * n1), lambda bi, j: (0, 0)),
                  pl.BlockSpec((None, 2, n1, SLAB, ch), lambda bi, j: (bi, 0, 0, j, 0)),
                  pl.BlockSpec((None, None, m, SLAB, ch), lambda bi, j: (lead, bi, 0, j, 0)),
                  pl.BlockSpec((None, None, m, SLAB, ch), lambda bi, j: (order, bi, 0, j, 0)),
                  pl.BlockSpec((None, 1, ch), lambda bi, j: (order, 0, 0))],
        out_specs=pl.BlockSpec((None, m, SLAB, ch), lambda bi, j: (bi, 0, j, 0)),
        out_shape=jax.ShapeDtypeStruct((b, m, n2, ch), F32),
        compiler_params=_params("parallel", "parallel"),
        name="hyena_dft_out",
    )(finv, bm, u, z, skip)


def _hy_small_kernel(z_ref, filt_ref, nrm_ref, fs_ref, finv_ref, sk_ref, o_ref):
    l = z_ref.shape[1]
    ch = z_ref.shape[2]
    fs = fs_ref[...]
    n = fs.shape[0] // 2
    spec = jnp.dot(fs, filt_ref[...], precision=HIGHEST, preferred_element_type=F32) / nrm_ref[...]
    sig = z_ref[2]
    for order in range(2):
        u = jnp.dot(fs[:, :l], sig, precision=HIGHEST, preferred_element_type=F32)
        hr = spec[:n, order * ch:(order + 1) * ch]
        hi = spec[n:, order * ch:(order + 1) * ch]
        yr, yi = _cmul(u[:n], u[n:], hr, hi)
        y = jnp.dot(finv_ref[...], jnp.concatenate([yr, yi], axis=0), precision=HIGHEST, preferred_element_type=F32)
        sig = z_ref[order] * (y + sk_ref[order:order + 1] * sig)
    o_ref[...] = sig


def _hy_small_call(z, filt, nrm, fs, finv, skip):
    _, b, l, ch = z.shape
    full = lambda a: pl.BlockSpec(a.shape, lambda bi: (0,) * a.ndim)
    return pl.pallas_call(
        _hy_small_kernel,
        grid=(b,),
        in_specs=[pl.BlockSpec((3, None, l, ch), lambda bi: (0, bi, 0, 0)),
                  full(filt), full(nrm), full(fs), full(finv), full(skip)],
        out_specs=pl.BlockSpec((None, l, ch), lambda bi: (bi, 0, 0)),
        out_shape=jax.ShapeDtypeStruct((b, l, ch), F32),
        compiler_params=_params("parallel"),
        name="hyena_short_sequence",
    )(z, filt, nrm, fs, finv, skip)


def _post_kernel(x_ref, ya_ref, yb_ref, hf_ref, hb_ref, gt_ref, yd_ref, mod_ref, wo_ref, g1_ref, g2_ref, g3_ref,
                 w1_ref, w2_ref, o_ref):
    m = mod_ref[...]
    gt = gt_ref[...]
    gelu = 0.5 * gt * (1.0 + jnp.tanh(math.sqrt(2.0 / math.pi) * (gt + 0.044715 * gt * gt * gt)))
    yc = (hf_ref[...] + hb_ref[...]) * gelu
    cat = jnp.concatenate([ya_ref[...].astype(BF16), yb_ref[...], yc.astype(BF16), yd_ref[...].astype(BF16)], axis=-1)
    mix = jnp.dot(cat, wo_ref[...], preferred_element_type=F32)
    x1 = x_ref[...] + m[2:3] * _rms(mix, g1_ref[...])
    h = (_rms(x1, g2_ref[...]) * (1.0 + m[4:5]) + m[3:4]).astype(BF16)
    dff = w1_ref.shape[1]
    fc = 1024
    acc = jnp.zeros(x1.shape, F32)
    for c in range(dff // fc):
        a = jnp.maximum(jnp.dot(h, w1_ref[:, c * fc:(c + 1) * fc], preferred_element_type=F32), 0.0)
        acc = acc + jnp.dot((a * a).astype(BF16), w2_ref[c * fc:(c + 1) * fc, :], preferred_element_type=F32)
    o_ref[...] = x1 + m[5:6] * _rms(acc, g3_ref[...])


def _post_call(x, ya, yb, hf, hb, p, yd, mod, wo, g1, g2, g3, w1, w2):
    b, l, d = x.shape
    ch = ya.shape[-1]
    tm = min(l, 512)
    row = lambda w, col=0: pl.BlockSpec((None, tm, w), lambda bi, i: (bi, i, col))
    vec = pl.BlockSpec((1, d), lambda bi, i: (0, 0))
    return pl.pallas_call(
        _post_kernel,
        grid=(b, l // tm),
        in_specs=[row(d), row(ch), row(ch), row(ch), row(ch), row(ch, 4), row(ch),
                  pl.BlockSpec((None, 6, d), lambda bi, i: (bi, 0, 0)),
                  _resident(wo.shape, lambda bi, i: (0, 0)), vec, vec, vec,
                  _resident(w1.shape, lambda bi, i: (0, 0)),
                  _resident(w2.shape, lambda bi, i: (0, 0))],
        out_specs=row(d),
        out_shape=jax.ShapeDtypeStruct((b, l, d), F32),
        compiler_params=_params("parallel", "parallel"),
        name="out_proj_mlp",
    )(x, ya, yb, hf, hb, p, yd, mod, wo, g1.reshape(1, d), g2.reshape(1, d), g3.reshape(1, d), w1, w2)


def _rope_tables(seq_len):
    t = jnp.arange(seq_len)
    inv = ROPE_THETA ** (-jnp.arange(ROPE_FREQS, dtype=F32) / ROPE_FREQS)
    ang_r = (t // GRID_W).astype(F32)[:, None] * inv
    ang_c = (t % GRID_W).astype(F32)[:, None] * inv
    cos = jnp.concatenate([jnp.cos(ang_r)] * 2 + [jnp.cos(ang_c)] * 2, axis=1)
    sin = jnp.concatenate([-jnp.sin(ang_r), jnp.sin(ang_r), -jnp.sin(ang_c), jnp.sin(ang_c)], axis=1)
    return jnp.tile(cos, (1, 2)), jnp.tile(sin, (1, 2))


def _hyena_embedding(seq_len):
    def circular(v):
        return jnp.concatenate([v, v[:1], v[:0:-1]])

    t01 = circular(jnp.linspace(0.0, 1.0, seq_len, dtype=F32))[:, None]
    w = circular(2.0 * math.pi * jnp.arange(seq_len, dtype=F32) / seq_len)
    bands = jnp.linspace(1e-4, HY_BANDS - 1, HY_BANDS, dtype=F32)
    z = w[:, None] * bands[None, :]
    pad = jnp.zeros((2 * seq_len, HY_EMB_PAD - HY_EMB), F32)
    return jnp.concatenate([t01, jnp.cos(z), -jnp.sin(z), pad], axis=-1)


def _dft_cos_sin(rows, cols, n):
    ang = (2.0 * math.pi / n) * ((jnp.arange(rows)[:, None] * jnp.arange(cols)[None, :]) % n).astype(F32)
    return jnp.cos(ang), -jnp.sin(ang)


def _dft_tables(seq_len):
    n = 2 * seq_len
    n2 = DFT_N2
    n1 = n // n2
    fr, fi = _dft_cos_sin(n1, n1, n1)
    f_outer = jnp.concatenate([fr, fi], axis=0).astype(BF16)
    f_inv = (jnp.concatenate([fr, fi], axis=1)[:n1 // 2] * (1.0 / n)).astype(BF16)
    f_inner = jnp.stack(_dft_cos_sin(n2, n2, n2))
    twiddle = jnp.stack(_dft_cos_sin(n1, n2, n))
    return f_outer, f_inv, f_inner, twiddle


def _dense_dft_tables(seq_len):
    n = 2 * seq_len
    fr, fi = _dft_cos_sin(n, n, n)
    fs = jnp.concatenate([fr, fi], axis=0)
    finv = jnp.concatenate([fr, fi], axis=1)[:seq_len] * (1.0 / n)
    return fs, finv


def _block_diag(w):
    g, c, _ = w.shape
    eye = jnp.eye(g, dtype=w.dtype)
    return (eye[:, None, :, None] * w[:, :, None, :]).reshape(g * c, g * c)


def _hyena_long(p, l_w, tables, filt, nrm):
    b, l, _ = p.shape
    f_outer, f_inv, f_inner, twiddle = tables
    n2 = DFT_N2
    n1 = 2 * l // n2
    ch = 256
    z = _hy_conv_call(p, l_w["hy_conv_w"], l_w["hy_conv_b"]).reshape(3, b, n1 // 2, n2, ch)
    fa = _dft_outer_call(f_outer, filt.reshape(1, 1, n1, n2, 2 * ch), 0)
    spec = _hy_spec_call(fa, twiddle, f_inner, nrm)
    skip = l_w["hy_skip"].reshape(2, 1, ch)
    sig, lead = z, 2
    for order in range(2):
        a = _dft_outer_call(f_outer[:, :n1 // 2], sig, lead)
        bm = _hy_mid_call(a, twiddle, f_inner, spec, order)
        sig, lead = _hy_out_call(f_inv, bm, sig, lead, z, order, skip)[None], 0
    return sig.reshape(b, l, ch)


def _layer_weights(args, l):
    return {k: v[l] for k, v in args.items()}


def kernel(x, c, ctx, c_ctx, w_mod, b_mod, g_pre_mix, g_post_mix, g_pre_mlp, g_post_mlp, w_in, w_out, pool_w, pool_scale, q_norm_g, k_norm_g, lru_conv_w, lru_conv_b, lru_wa, lru_ba, lru_wx, lru_bx, lru_lambda, hy_conv_w, hy_conv_b, hy_w1, hy_b1, hy_freq, hy_w2, hy_b2, hy_w3, hy_skip, mlp_w1, mlp_w2):
    b, l, d = x.shape
    lc = ctx.shape[1]
    depth = w_mod.shape[0]
    per_layer = dict(g_pre_mix=g_pre_mix, g_post_mix=g_post_mix, g_pre_mlp=g_pre_mlp, g_post_mlp=g_post_mlp,
                     w_in=w_in.astype(BF16), w_out=w_out.astype(BF16), pool_w=pool_w, pool_scale=pool_scale,
                     q_norm_g=q_norm_g, k_norm_g=k_norm_g, lru_conv_w=lru_conv_w, lru_conv_b=lru_conv_b,
                     lru_wa=lru_wa, lru_ba=lru_ba, lru_wx=lru_wx, lru_bx=lru_bx, lru_lambda=lru_lambda,
                     hy_conv_w=hy_conv_w, hy_conv_b=hy_conv_b, hy_w1=hy_w1, hy_b1=hy_b1, hy_freq=hy_freq,
                     hy_w2=hy_w2, hy_b2=hy_b2, hy_w3=hy_w3, hy_skip=hy_skip,
                     mlp_w1=mlp_w1.astype(BF16), mlp_w2=mlp_w2.astype(BF16))

    cos, sin = _rope_tables(l)
    avg = _block_diag(jnp.full((2, HEAD_DIM, HEAD_DIM), 1.0 / HEAD_DIM, F32)).astype(BF16)
    emb_x, emb_c = _hyena_embedding(l), _hyena_embedding(lc)
    dft_x = _dft_tables(l)
    fs_c, finv_c = _dense_dft_tables(lc)
    deltas = jnp.abs(jnp.linspace(HY_MIN_DECAY, HY_MAX_DECAY, 256, dtype=F32))
    deltas2 = jnp.tile(deltas, 2).reshape(1, 512)

    cond = jnp.zeros((8, d), F32).at[:b].set(c).at[b].set(c_ctx)
    mods = _mod_call(cond, w_mod, b_mod).reshape(depth, 8, 6, d)

    cx = ctx
    for li in range(depth):
        w = _layer_weights(per_layer, li)
        need_ctx = li < depth - 1
        mod_x = mods[li, :b]
        mod_c = jnp.broadcast_to(mods[li, b:b + 1], (b, 6, d))
        w1p = jnp.pad(w["hy_w1"], ((0, HY_EMB_PAD - HY_EMB), (0, 0)))
        pool_bd = _block_diag(w["pool_w"]).astype(BF16)
        gq = jnp.tile(w["q_norm_g"], 2).reshape(1, 128)
        gk = jnp.tile(w["k_norm_g"], 2).reshape(1, 128)
        wa = [_block_diag(w["lru_wa"][dr]).astype(BF16) for dr in range(2)]
        wx = [_block_diag(w["lru_wx"][dr]).astype(BF16) for dr in range(2)]

        px = _inproj_call(x, mod_x, w["g_pre_mix"], w["w_in"])
        pc = _inproj_call(cx, mod_c, w["g_pre_mix"], w["w_in"])

        qtx, kx, vtx = _qkv_call(px, cos, sin, gq, gk, avg)
        qtc, kc, vtc = _qkv_call(pc, None, None, gq, gk, avg)
        yb_x = _flash_call(qtx, kx, vtx, kc, vtc)

        hs_c, hs_x = [], []
        for dr in range(2):
            lw = (w["lru_conv_w"], w["lru_conv_b"], wa[dr], w["lru_ba"][dr], wx[dr], w["lru_bx"][dr],
                  w["lru_lambda"][dr])
            hc = _lru_call(pc, jnp.zeros((b, 1, 256), F32), *lw, reverse=bool(dr))
            h0 = hc[:, :1] if dr else hc[:, -1:]
            hs_c.append(hc)
            hs_x.append(_lru_call(px, h0, *lw, reverse=bool(dr)))

        ya_x = _pool_call(px, pool_bd, w["pool_scale"])
        hy = (w1p, w["hy_b1"], w["hy_freq"], w["hy_w2"], w["hy_b2"], w["hy_w3"], deltas2)
        filt_x, nrm_x = _hy_filter_call(emb_x, *hy, seq_len=l)
        yd_x = _hyena_long(px, w, dft_x, filt_x, nrm_x)

        x = _post_call(x, ya_x, yb_x, hs_x[0], hs_x[1], px, yd_x, mod_x, w["w_out"], w["g_post_mix"],
                       w["g_pre_mlp"], w["g_post_mlp"], w["mlp_w1"], w["mlp_w2"])

        if need_ctx:
            yb_c = _flash_call(qtc, kc, vtc)
            ya_c = _pool_call(pc, pool_bd, w["pool_scale"])
            filt_c, nrm_c = _hy_filter_call(emb_c, *hy, seq_len=lc)
            z_c = _hy_conv_call(pc, w["hy_conv_w"], w["hy_conv_b"])
            yd_c = _hy_small_call(z_c, filt_c, nrm_c, fs_c, finv_c, w["hy_skip"])
            cx = _post_call(cx, ya_c, yb_c, hs_c[0], hs_c[1], pc, yd_c, mod_c, w["w_out"], w["g_post_mix"],
                            w["g_pre_mlp"], w["g_post_mlp"], w["mlp_w1"], w["mlp_w2"])
    return x
```
